```python
import math
import jax, jax.numpy as jnp
from jax import lax
import numpy as np

D_MODEL = 1024
BATCH = 8
SEQ = 4096
DEPTH = 4

CTX_LEN = 256
GRID_W = 64
CHUNK = 64
CONV_K = 5
N_BRANCH = 4
MIX_W = D_MODEL // 2
SSD_INNER = MIX_W
SSD_HEAD_DIM = 64
SSD_HEADS = SSD_INNER // SSD_HEAD_DIM
SSD_STATE = 128
SSD_GROUPS = 2
SSD_BC = SSD_GROUPS * SSD_STATE
MLSTM_HEADS = 4
MLSTM_INNER = MIX_W
MLSTM_V = MLSTM_INNER // MLSTM_HEADS
MLSTM_K = MLSTM_V // 2
GDN_HEADS = 4
GDN_INNER = MIX_W
GDN_HEAD_DIM = GDN_INNER // GDN_HEADS
GLA_HEADS = 4
GLA_INNER = MIX_W
GLA_V = GLA_INNER // GLA_HEADS
GLA_K = GLA_V // 2
GLA_RANK = 16
GLA_GATE_NORM = 16
FFN_HIDDEN = -(-(8 * D_MODEL // 3) // 256) * 256
SSD_COLS = 2 * SSD_INNER + 2 * SSD_BC + 2 * SSD_HEADS
MLSTM_COLS = 2 * MLSTM_HEADS * MLSTM_K + 2 * MLSTM_INNER + 4 * MLSTM_HEADS
GDN_COLS = 4 * GDN_INNER + 4 * GDN_HEADS
GLA_COLS = 2 * GLA_HEADS * GLA_K + 2 * GLA_INNER + 2 * GLA_RANK
GATE_COLS = N_BRANCH * D_MODEL
W_IN_COLS = SSD_COLS + MLSTM_COLS + GDN_COLS + GLA_COLS + GATE_COLS

kernel_name = 'hybrid_ssd_mlstm_gdn_gla_prefix_dit'


def split_cols(x, sizes):
    return jnp.split(x, np.cumsum(sizes)[:-1].tolist(), axis=-1)


def rmsnorm(x, w, eps=1e-6):
    xf = x.astype(jnp.float32)
    y = xf * lax.rsqrt(jnp.mean(xf * xf, axis=-1, keepdims=True) + eps)
    return (y * w.astype(jnp.float32)).astype(x.dtype)


def group_rmsnorm(x, w, groups, eps=1e-6):
    xf = x.astype(jnp.float32)
    shp = xf.shape
    xg = xf.reshape(shp[:-1] + (groups, shp[-1] // groups))
    xg = xg * lax.rsqrt(jnp.mean(xg * xg, axis=-1, keepdims=True) + eps)
    return xg.reshape(shp) * w.astype(jnp.float32)


def l2norm(x, eps=1e-6):
    xf = x.astype(jnp.float32)
    return xf * lax.rsqrt(jnp.sum(xf * xf, axis=-1, keepdims=True) + eps)


def modulate(x, shift, scale):
    return x * (1 + scale) + shift


def swiglu(h, w_gate, w_up, w_down):
    return (jax.nn.silu(h @ w_gate) * (h @ w_up)) @ w_down


def dwconv_centred(x, w, b=None):
    k = w.shape[0]
    y = lax.conv_general_dilated(x, w[:, None, :].astype(x.dtype), window_strides=(1,),
                                 padding=[(k // 2, k // 2)], dimension_numbers=('NWC', 'WIO', 'NWC'),
                                 feature_group_count=x.shape[-1])
    return y if b is None else y + b


def to_colmajor(x, rows):
    bsz, t, ch = x.shape
    return x.reshape(bsz, rows, GRID_W, ch).transpose(0, 2, 1, 3).reshape(bsz, t, ch)


def from_colmajor(x, rows):
    bsz, t, ch = x.shape
    return x.reshape(bsz, GRID_W, rows, ch).transpose(0, 2, 1, 3).reshape(bsz, t, ch)


def _chunks(a):
    bsz, t = a.shape[:2]
    return jnp.moveaxis(a.astype(jnp.float32).reshape((bsz, t // CHUNK, CHUNK) + a.shape[2:]), 1, 0)


def _unchunks(a):
    n, bsz = a.shape[:2]
    return jnp.moveaxis(a, 0, 1).reshape((bsz, n * CHUNK) + a.shape[3:])


def ssd_scan(q, k, v, log_a, state):
    mask = jnp.tril(jnp.ones((CHUNK, CHUNK), bool))

    def step(s, inp):
        qc, kc, vc, ac = inp
        cum = jnp.cumsum(ac, axis=1)
        cum_h = jnp.swapaxes(cum, 1, 2)
        decay = jnp.exp(jnp.where(mask, cum_h[..., :, None] - cum_h[..., None, :], -jnp.inf))
        scores = jnp.einsum('bthn,bshn->bhts', qc, kc) * decay
        y = (jnp.einsum('bhts,bshp->bthp', scores, vc)
             + jnp.einsum('bthn,bhnp->bthp', qc * jnp.exp(cum)[..., None], s))
        last = cum[:, -1]
        k_end = kc * jnp.exp(last[:, None] - cum)[..., None]
        s = s * jnp.exp(last)[..., None, None] + jnp.einsum('bshn,bshp->bhnp', k_end, vc)
        return s, y

    s, y = lax.scan(step, state, (_chunks(q), _chunks(k), _chunks(v), _chunks(log_a)))
    return _unchunks(y), s


def gla_scan(q, k, v, log_g, state):
    mask = jnp.tril(jnp.ones((CHUNK, CHUNK), bool))

    def step(s, inp):
        qc, kc, vc, gc = inp
        cum = jnp.cumsum(gc, axis=1)
        ref = cum[:, CHUNK // 2:CHUNK // 2 + 1]
        scores = jnp.einsum('bthk,bshk->bhts', qc * jnp.exp(cum - ref), kc * jnp.exp(ref - cum))
        scores = jnp.where(mask, scores, 0.0)
        y = (jnp.einsum('bhts,bshv->bthv', scores, vc)
             + jnp.einsum('bthk,bhkv->bthv', qc * jnp.exp(cum), s))
        last = cum[:, -1]
        s = s * jnp.exp(last)[..., None] + jnp.einsum('bshk,bshv->bhkv', kc * jnp.exp(last[:, None] - cum), vc)
        return s, y

    s, y = lax.scan(step, state, (_chunks(q), _chunks(k), _chunks(v), _chunks(log_g)))
    return _unchunks(y), s


def mlstm_scan(q, k, v, log_f, log_i, state):
    mask = jnp.tril(jnp.ones((CHUNK, CHUNK), bool))

    def step(carry, inp):
        cmat, nvec, m = carry
        qc, kc, vc, fc, ic = inp
        b = jnp.swapaxes(jnp.cumsum(fc, axis=1), 1, 2)
        ih = jnp.swapaxes(ic, 1, 2)
        d_log = jnp.where(mask, b[..., :, None] - b[..., None, :] + ih[..., None, :], -jnp.inf)
        inter = b + m[..., None]
        m_t = jnp.maximum(inter, jnp.max(d_log, axis=-1))
        scores = jnp.einsum('bthk,bshk->bhts', qc, kc) * jnp.exp(d_log - m_t[..., None])
        w_inter = jnp.swapaxes(jnp.exp(inter - m_t), 1, 2)
        num = (jnp.einsum('bhts,bshv->bthv', scores, vc)
               + jnp.einsum('bthk,bhkv->bthv', qc, cmat) * w_inter[..., None])
        den = jnp.swapaxes(jnp.sum(scores, axis=-1), 1, 2) + jnp.einsum('bthk,bhk->bth', qc, nvec) * w_inter
        floor = jnp.swapaxes(jnp.exp(-m_t), 1, 2)
        h = num / jnp.maximum(jnp.abs(den), floor)[..., None]
        b_last = b[..., -1]
        end_log = b_last[..., None] - b + ih
        m_new = jnp.maximum(b_last + m, jnp.max(end_log, axis=-1))
        w_end = jnp.exp(end_log - m_new[..., None])
        keep = jnp.exp(b_last + m - m_new)
        cmat = cmat * keep[..., None, None] + jnp.einsum('bshk,bhs,bshv->bhkv', kc, w_end, vc)
        nvec = nvec * keep[..., None] + jnp.einsum('bshk,bhs->bhk', kc, w_end)
        return (cmat, nvec, m_new), h

    st, h = lax.scan(step, state, (_chunks(q), _chunks(k), _chunks(v), _chunks(log_f), _chunks(log_i)))
    return _unchunks(h), st


def gdn_scan(q, k, v, log_a, beta, state):
    mask = jnp.tril(jnp.ones((CHUNK, CHUNK), bool))
    strict = jnp.tril(jnp.ones((CHUNK, CHUNK), bool), -1)
    dk = q.shape[-1]

    def step(s, inp):
        qc, kc, vc, gc, bc = inp
        cum = jnp.cumsum(gc, axis=1)
        cum_h = jnp.swapaxes(cum, 1, 2)
        decay = jnp.exp(jnp.where(mask, cum_h[..., :, None] - cum_h[..., None, :], -jnp.inf))
        kb = kc * bc[..., None]
        a_strict = jnp.where(strict, jnp.einsum('bthk,bshk->bhts', kb, kc) * decay, 0.0)
        rhs = jnp.concatenate([kb * jnp.exp(cum)[..., None], vc * bc[..., None]], axis=-1)
        sol = lax.linalg.triangular_solve(a_strict, jnp.swapaxes(rhs, 1, 2), left_side=True,
                                          lower=True, unit_diagonal=True)
        w, u = sol[..., :dk], sol[..., dk:]
        v_new = u - jnp.einsum('bhtk,bhkv->bhtv', w, s)
        attn = jnp.einsum('bthk,bshk->bhts', qc, kc) * decay
        y = (jnp.einsum('bthk,bhkv->bthv', qc * jnp.exp(cum)[..., None], s)
             + jnp.einsum('bhts,bhsv->bthv', attn, v_new))
        last = cum[:, -1]
        k_end = kc * jnp.exp(last[:, None] - cum)[..., None]
        s = s * jnp.exp(last)[..., None, None] + jnp.einsum('bshk,bhsv->bhkv', k_end, v_new)
        return s, y

    s, y = lax.scan(step, state, (_chunks(q), _chunks(k), _chunks(v), _chunks(log_a), _chunks(beta)))
    return _unchunks(y), s


def prefixed_scan(scan_fn, ctx_args, lat_args, init_state, reverse):
    if reverse:
        ctx_args = tuple(jnp.flip(a, axis=1) for a in ctx_args)
        lat_args = tuple(jnp.flip(a, axis=1) for a in lat_args)
    y_ctx, ctx_state = scan_fn(*ctx_args, init_state)
    y_lat, _ = scan_fn(*lat_args, ctx_state)
    if reverse:
        y_ctx, y_lat = jnp.flip(y_ctx, axis=1), jnp.flip(y_lat, axis=1)
    return y_ctx, y_lat


def ssd_mixer(p_ctx, p_lat, conv_w, conv_b, dt_bias, a_log, d_skip, norm_w):
    def prep(p):
        bsz, t, _ = p.shape
        z, xbc, dt = split_cols(p, [SSD_INNER, SSD_INNER + 2 * SSD_BC, 2 * SSD_HEADS])
        xbc = jax.nn.silu(dwconv_centred(xbc, conv_w, conv_b))
        xs, bm, cm = split_cols(xbc, [SSD_INNER, SSD_BC, SSD_BC])
        rep = SSD_HEADS // SSD_GROUPS
        bm = jnp.repeat(bm.reshape(bsz, t, SSD_GROUPS, SSD_STATE), rep, axis=2)
        cm = jnp.repeat(cm.reshape(bsz, t, SSD_GROUPS, SSD_STATE), rep, axis=2)
        return z, xs.reshape(bsz, t, SSD_HEADS, SSD_HEAD_DIM), bm, cm, dt.reshape(bsz, t, 2, SSD_HEADS)

    def dir_args(st, d):
        _, xs, bm, cm, dt = st
        delta = jax.nn.softplus(dt[:, :, d] + dt_bias[d])
        return (cm, bm, xs * delta[..., None], -jnp.exp(a_log[d]) * delta)

    streams = [prep(p_ctx), prep(p_lat)]
    ys = [st[1] * d_skip[:, None] for st in streams]
    bsz = p_lat.shape[0]
    for d in range(2):
        init = jnp.zeros((bsz, SSD_HEADS, SSD_STATE, SSD_HEAD_DIM), jnp.float32)
        y_c, y_l = prefixed_scan(ssd_scan, dir_args(streams[0], d), dir_args(streams[1], d), init, d == 1)
        ys = [ys[0] + y_c, ys[1] + y_l]
    outs = []
    for st, y, p in zip(streams, ys, (p_ctx, p_lat)):
        bsz, t, _ = p.shape
        gated = y.reshape(bsz, t, SSD_INNER) * jax.nn.silu(st[0]).astype(jnp.float32)
        outs.append(group_rmsnorm(gated, norm_w, SSD_GROUPS).astype(p.dtype))
    return outs[0], outs[1]


def mlstm_mixer(p_ctx, p_lat, i_bias, f_bias, norm_w):
    def prep(p):
        bsz, t, _ = p.shape
        q, k, v, o, ig, fg = split_cols(p, [MLSTM_HEADS * MLSTM_K, MLSTM_HEADS * MLSTM_K, MLSTM_INNER,
                                            MLSTM_INNER, 2 * MLSTM_HEADS, 2 * MLSTM_HEADS])
        q = q.reshape(bsz, t, MLSTM_HEADS, MLSTM_K)
        k = k.reshape(bsz, t, MLSTM_HEADS, MLSTM_K) * MLSTM_K ** -0.5
        v = v.reshape(bsz, t, MLSTM_HEADS, MLSTM_V)
        log_i = ig.reshape(bsz, t, 2, MLSTM_HEADS) + i_bias
        log_f = jax.nn.log_sigmoid(fg.reshape(bsz, t, 2, MLSTM_HEADS) + f_bias)
        return o, q, k, v, log_f, log_i

    def dir_args(st, d):
        _, q, k, v, log_f, log_i = st
        return (q, k, v, log_f[:, :, d], log_i[:, :, d])

    streams = [prep(p_ctx), prep(p_lat)]
    bsz = p_lat.shape[0]
    ys = [0.0, 0.0]
    for d in range(2):
        init = (jnp.zeros((bsz, MLSTM_HEADS, MLSTM_K, MLSTM_V), jnp.float32),
                jnp.zeros((bsz, MLSTM_HEADS, MLSTM_K), jnp.float32),
                jnp.zeros((bsz, MLSTM_HEADS), jnp.float32))
        y_c, y_l = prefixed_scan(mlstm_scan, dir_args(streams[0], d), dir_args(streams[1], d), init, d == 1)
        ys = [ys[0] + y_c, ys[1] + y_l]
    outs = []
    for st, y, p in zip(streams, ys, (p_ctx, p_lat)):
        bsz, t, _ = p.shape
        hn = group_rmsnorm(y.reshape(bsz, t, MLSTM_INNER), norm_w, MLSTM_HEADS)
        outs.append((hn * jax.nn.sigmoid(st[0]).astype(jnp.float32)).astype(p.dtype))
    return outs[0], outs[1]


def gdn_mixer(p_ctx, p_lat, conv_w, a_log, dt_bias, norm_w):
    def prep(p):
        bsz, t, _ = p.shape
        qkv, g, a, b = split_cols(p, [3 * GDN_INNER, GDN_INNER, 2 * GDN_HEADS, 2 * GDN_HEADS])
        qkv = jax.nn.silu(dwconv_centred(qkv, conv_w))
        q, k, v = [u.reshape(bsz, t, GDN_HEADS, GDN_HEAD_DIM) for u in split_cols(qkv, [GDN_INNER] * 3)]
        q = l2norm(q) * GDN_HEAD_DIM ** -0.5
        k = l2norm(k)
        log_a = -jnp.exp(a_log) * jax.nn.softplus(a.reshape(bsz, t, 2, GDN_HEADS) + dt_bias)
        beta = jax.nn.sigmoid(b.reshape(bsz, t, 2, GDN_HEADS))
        return g, q, k, v, log_a, beta

    def dir_args(st, d):
        _, q, k, v, log_a, beta = st
        return (q, k, v, log_a[:, :, d], beta[:, :, d])

    streams = [prep(p_ctx), prep(p_lat)]
    bsz = p_lat.shape[0]
    ys = [0.0, 0.0]
    for d in range(2):
        init = jnp.zeros((bsz, GDN_HEADS, GDN_HEAD_DIM, GDN_HEAD_DIM), jnp.float32)
        y_c, y_l = prefixed_scan(gdn_scan, dir_args(streams[0], d), dir_args(streams[1], d), init, d == 1)
        ys = [ys[0] + y_c, ys[1] + y_l]
    outs = []
    for st, y, p in zip(streams, ys, (p_ctx, p_lat)):
        bsz, t, _ = p.shape
        on = group_rmsnorm(y.reshape(bsz, t, GDN_INNER), norm_w, GDN_HEADS)
        outs.append((on * jax.nn.silu(st[0]).astype(jnp.float32)).astype(p.dtype))
    return outs[0], outs[1]


def gla_mixer(p_ctx, p_lat, gk_up, gk_bias, norm_w):
    def prep(p):
        bsz, t, _ = p.shape
        q, k, v, g, gk = split_cols(p, [GLA_HEADS * GLA_K, GLA_HEADS * GLA_K, GLA_INNER, GLA_INNER, 2 * GLA_RANK])
        q = q.reshape(bsz, t, GLA_HEADS, GLA_K) * GLA_K ** -0.5
        k = k.reshape(bsz, t, GLA_HEADS, GLA_K)
        v = v.reshape(bsz, t, GLA_HEADS, GLA_V)
        gk = jnp.einsum('btdr,drk->btdk', gk.reshape(bsz, t, 2, GLA_RANK), gk_up) + gk_bias
        log_g = (jax.nn.log_sigmoid(gk.astype(jnp.float32)) / GLA_GATE_NORM).reshape(bsz, t, 2, GLA_HEADS, GLA_K)
        return g, q, k, v, log_g

    def dir_args(st, d):
        _, q, k, v, log_g = st
        return (q, k, v, log_g[:, :, d])

    streams = [prep(p_ctx), prep(p_lat)]
    bsz = p_lat.shape[0]
    ys = [0.0, 0.0]
    for d in range(2):
        init = jnp.zeros((bsz, GLA_HEADS, GLA_K, GLA_V), jnp.float32)
        y_c, y_l = prefixed_scan(gla_scan, dir_args(streams[0], d), dir_args(streams[1], d), init, d == 1)
        ys = [ys[0] + y_c, ys[1] + y_l]
    outs = []
    for st, y, p in zip(streams, ys, (p_ctx, p_lat)):
        bsz, t, _ = p.shape
        on = group_rmsnorm(y.reshape(bsz, t, GLA_INNER), norm_w, GLA_HEADS)
        outs.append((on * jax.nn.silu(st[0]).astype(jnp.float32)).astype(p.dtype))
    return outs[0], outs[1]


def merge_branches(ys, gate_logits, w_branch, w_out):
    gates = split_cols(gate_logits, [D_MODEL] * N_BRANCH)
    u = jax.nn.sigmoid(gates[0]) * (ys[0] @ w_branch[0])
    for i in range(1, N_BRANCH):
        u = u + jax.nn.sigmoid(gates[i]) * (ys[i] @ w_branch[i])
    return u @ w_out


def setup_inputs(seed: int = 0) -> dict:
    key = jax.random.key(seed)
    ks = iter(jax.random.split(key, 40))

    def nrm(shape, scale):
        return jax.random.normal(next(ks), shape, jnp.float32) * scale

    def gain(shape):
        return 1.0 + nrm(shape, 0.02)

    def unif(shape, lo, hi):
        return jax.random.uniform(next(ks), shape, jnp.float32, lo, hi)

    def dt_bias(shape):
        dt = jnp.exp(unif(shape, math.log(1e-3), math.log(1e-1)))
        return dt + jnp.log(-jnp.expm1(-dt))

    d = D_MODEL
    return {
        'x': nrm((BATCH, SEQ, d), 1.0),
        'c': nrm((BATCH, d), 1.0),
        'ctx': nrm((BATCH, CTX_LEN, d), 1.0),
        'c_ctx': nrm((d,), 1.0),
        'ada_w': nrm((DEPTH, d, 6 * d), 0.5 * d ** -0.5),
        'ada_b': nrm((DEPTH, 6 * d), 0.02),
        'norm_mix_w': gain((DEPTH, d)),
        'w_in': nrm((DEPTH, d, W_IN_COLS), d ** -0.5),
        'ssd_conv_w': nrm((DEPTH, CONV_K, SSD_INNER + 2 * SSD_BC), CONV_K ** -0.5),
        'ssd_conv_b': nrm((DEPTH, SSD_INNER + 2 * SSD_BC), 0.02),
        'ssd_dt_bias': dt_bias((DEPTH, 2, SSD_HEADS)),
        'ssd_a_log': jnp.log(unif((DEPTH, 2, SSD_HEADS), 1.0, 16.0)),
        'ssd_d': gain((DEPTH, SSD_HEADS)),
        'ssd_norm_w': gain((DEPTH, SSD_INNER)),
        'mlstm_i_bias': nrm((DEPTH, 2, MLSTM_HEADS), 0.1),
        'mlstm_f_bias': jnp.linspace(3.0, 6.0, MLSTM_HEADS) + nrm((DEPTH, 2, MLSTM_HEADS), 0.1),
        'mlstm_norm_w': gain((DEPTH, MLSTM_INNER)),
        'gdn_conv_w': nrm((DEPTH, CONV_K, 3 * GDN_INNER), CONV_K ** -0.5),
        'gdn_a_log': jnp.log(unif((DEPTH, 2, GDN_HEADS), 1.0, 16.0)),
        'gdn_dt_bias': dt_bias((DEPTH, 2, GDN_HEADS)),
        'gdn_norm_w': gain((DEPTH, GDN_INNER)),
        'gla_gk_up': nrm((DEPTH, 2, GLA_RANK, GLA_HEADS * GLA_K), GLA_RANK ** -0.5),
        'gla_gk_bias': nrm((DEPTH, 2, GLA_HEADS * GLA_K), 0.02),
        'gla_norm_w': gain((DEPTH, GLA_INNER)),
        'w_branch': nrm((DEPTH, N_BRANCH, MIX_W, d), MIX_W ** -0.5),
        'w_out': nrm((DEPTH, d, d), d ** -0.5),
        'norm_ffn_w': gain((DEPTH, d)),
        'ffn_w_gate': nrm((DEPTH, d, FFN_HIDDEN), d ** -0.5),
        'ffn_w_up': nrm((DEPTH, d, FFN_HIDDEN), d ** -0.5),
        'ffn_w_down': nrm((DEPTH, FFN_HIDDEN, d), FFN_HIDDEN ** -0.5),
        'norm_f_w': gain((d,)),
    }


def reference(x, c, ctx, c_ctx, ada_w, ada_b, norm_mix_w, w_in, ssd_conv_w, ssd_conv_b, ssd_dt_bias,
              ssd_a_log, ssd_d, ssd_norm_w, mlstm_i_bias, mlstm_f_bias, mlstm_norm_w, gdn_conv_w, gdn_a_log,
              gdn_dt_bias, gdn_norm_w, gla_gk_up, gla_gk_bias, gla_norm_w, w_branch, w_out, norm_ffn_w,
              ffn_w_gate, ffn_w_up, ffn_w_down, norm_f_w):
    rows = x.shape[1] // GRID_W
    cond_lat = jax.nn.silu(c)[:, None, :]
    cond_ctx = jax.nn.silu(c_ctx)[None, None, :]
    cols = [SSD_COLS, MLSTM_COLS, GDN_COLS, GLA_COLS, GATE_COLS]
    h_lat, h_ctx = x, ctx
    for l in range(DEPTH):
        mod_l = split_cols(cond_lat @ ada_w[l] + ada_b[l], [D_MODEL] * 6)
        mod_c = split_cols(cond_ctx @ ada_w[l] + ada_b[l], [D_MODEL] * 6)
        p_l = split_cols(modulate(rmsnorm(h_lat, norm_mix_w[l]), mod_l[0], mod_l[1]) @ w_in[l], cols)
        p_c = split_cols(modulate(rmsnorm(h_ctx, norm_mix_w[l]), mod_c[0], mod_c[1]) @ w_in[l], cols)
        ssd_c, ssd_l = ssd_mixer(p_c[0], p_l[0], ssd_conv_w[l], ssd_conv_b[l], ssd_dt_bias[l], ssd_a_log[l],
                                 ssd_d[l], ssd_norm_w[l])
        ml_c, ml_l = mlstm_mixer(p_c[1], to_colmajor(p_l[1], rows), mlstm_i_bias[l], mlstm_f_bias[l],
                                 mlstm_norm_w[l])
        ml_l = from_colmajor(ml_l, rows)
        gd_c, gd_l = gdn_mixer(p_c[2], to_colmajor(p_l[2], rows), gdn_conv_w[l], gdn_a_log[l], gdn_dt_bias[l],
                               gdn_norm_w[l])
        gd_l = from_colmajor(gd_l, rows)
        gl_c, gl_l = gla_mixer(p_c[3], p_l[3], gla_gk_up[l], gla_gk_bias[l], gla_norm_w[l])
        h_lat = h_lat + mod_l[2] * merge_branches((ssd_l, ml_l, gd_l, gl_l), p_l[4], w_branch[l], w_out[l])
        h_lat = h_lat + mod_l[5] * swiglu(modulate(rmsnorm(h_lat, norm_ffn_w[l]), mod_l[3], mod_l[4]),
                                          ffn_w_gate[l], ffn_w_up[l], ffn_w_down[l])
        if l < DEPTH - 1:
            h_ctx = h_ctx + mod_c[2] * merge_branches((ssd_c, ml_c, gd_c, gl_c), p_c[4], w_branch[l], w_out[l])
            h_ctx = h_ctx + mod_c[5] * swiglu(modulate(rmsnorm(h_ctx, norm_ffn_w[l]), mod_c[3], mod_c[4]),
                                              ffn_w_gate[l], ffn_w_up[l], ffn_w_down[l])
    return rmsnorm(h_lat, norm_f_w)
```

```python
import functools

import jax
import jax.numpy as jnp
from jax import lax
from jax.experimental import pallas as pl
from jax.experimental.pallas import tpu as pltpu

F32 = jnp.float32
BF16 = jnp.bfloat16
HIGHEST = lax.Precision.HIGHEST

D_MODEL = 1024
GRID_W = 64
CHUNK = 64
CHUNK_SHIFT = CHUNK.bit_length() - 1
CONV_K = 5
CONV_HALO = 8
MIX_W = 512
LANES = 128
TOKEN_BLOCK = 256
FFN_HIDDEN = 2816
VMEM_LIMIT = 56 * 1024 * 1024

SSD_HEADS, SSD_HEAD_DIM, SSD_STATE, SSD_GROUPS = 8, 64, 128, 2
MLSTM_HEADS, MLSTM_K, MLSTM_V = 4, 64, 128
GDN_HEADS, GDN_HEAD_DIM = 4, 128
GLA_HEADS, GLA_K, GLA_V, GLA_RANK, GLA_GATE_NORM = 4, 64, 128, 16, 16

SSD_P = 2 * MIX_W + 2 * SSD_GROUPS * SSD_STATE + LANES
MLSTM_P = 2 * MLSTM_HEADS * MLSTM_K + 2 * MIX_W + LANES
GDN_P = 4 * MIX_W + LANES
GLA_P = 2 * GLA_HEADS * GLA_K + 2 * MIX_W + LANES


def _mm(a, b):
    return jnp.dot(a.astype(BF16), b.astype(BF16), preferred_element_type=F32)


def _mm_nt(a, b):
    return lax.dot_general(a.astype(BF16), b.astype(BF16), (((1,), (1,)), ((), ())),
                           preferred_element_type=F32)


def _mm_tn(a, b):
    return lax.dot_general(a.astype(BF16), b.astype(BF16), (((0,), (0,)), ((), ())),
                           preferred_element_type=F32)


def _mm_exact(a, b):
    return jnp.dot(a, b, precision=HIGHEST, preferred_element_type=F32)


def _rows_of(sel, x):
    return lax.dot_general(sel, x, (((1,), (1,)), ((), ())), precision=HIGHEST,
                           preferred_element_type=F32)


def _sigmoid(x):
    return 1.0 / (1.0 + jnp.exp(-x))


def _silu(x):
    return x * _sigmoid(x)


def _softplus(x):
    return jnp.maximum(x, 0.0) + jnp.log1p(jnp.exp(-jnp.abs(x)))


def _log_sigmoid(x):
    return jnp.minimum(x, 0.0) - jnp.log1p(jnp.exp(-jnp.abs(x)))


def _iota2(shape, axis):
    return lax.broadcasted_iota(jnp.int32, shape, axis)


def _scan_mask(n, d, strict=False):
    r, c = _iota2((n, n), 0), _iota2((n, n), 1)
    if d == 0:
        return (c < r) if strict else (c <= r)
    return (c > r) if strict else (c >= r)


def _block_cumsum_matrix(n, d):
    r, c = _iota2((n, n), 0), _iota2((n, n), 1)
    same = (r >> CHUNK_SHIFT) == (c >> CHUNK_SHIFT)
    tri = (c <= r) if d == 0 else (c >= r)
    return jnp.where(same & tri, 1.0, 0.0).astype(F32)


def _one_hot_rows(nrows, lane0):
    r, c = _iota2((nrows, LANES), 0), _iota2((nrows, LANES), 1)
    return jnp.where(c == r + lane0, 1.0, 0.0).astype(F32)


def _rev_block(i, nblk, nctx):
    return jnp.where(i < nctx, nctx - 1 - i, nblk - 1 - (i - nctx))


def _chunk_rows(c, d, nch):
    r0 = c * CHUNK if d == 0 else (nch - 1 - c) * CHUNK
    return pl.ds(pl.multiple_of(r0, CHUNK), CHUNK)


def _group_rmsnorm(x, w, groups, eps=1e-6):
    gw = x.shape[-1] // groups
    outs = []
    for g in range(groups):
        xg = x[:, g * gw:(g + 1) * gw]
        ms = jnp.mean(xg * xg, axis=-1, keepdims=True)
        outs.append(xg * lax.rsqrt(ms + eps))
    return jnp.concatenate(outs, axis=-1) * w


def _norm_modulate(x, nw, shift, scale, eps=1e-6):
    ms = jnp.mean(x * x, axis=-1, keepdims=True)
    y = x * lax.rsqrt(ms + eps) * nw
    return y * (1.0 + scale) + shift


def _conv_silu(dst_ref, d, xpad_ref, p_ref, prev_ref, next_ref, col0, width, w_ref, b_ref,
               left_valid, right_valid):
    tb = p_ref.shape[0]
    for s in range(width // LANES):
        cols = slice(col0 + s * LANES, col0 + (s + 1) * LANES)
        ocols = slice(s * LANES, (s + 1) * LANES)
        xpad_ref[0:CONV_HALO, :] = jnp.where(left_valid, prev_ref[:, cols], 0.0)
        xpad_ref[CONV_HALO:CONV_HALO + tb, :] = p_ref[:, cols]
        xpad_ref[CONV_HALO + tb:2 * CONV_HALO + tb, :] = jnp.where(right_valid, next_ref[:, cols], 0.0)
        base = CONV_HALO - CONV_K // 2
        acc = w_ref[0:1, ocols] * xpad_ref[base:base + tb, :]
        for k in range(1, CONV_K):
            acc = acc + w_ref[k:k + 1, ocols] * xpad_ref[base + k:base + k + tb, :]
        if b_ref is not None:
            acc = acc + b_ref[:, ocols]
        dst_ref[d, :, ocols] = _silu(acc)


def _conv_valid(j, nblk, nctx):
    left = jnp.logical_and(j != 0, j != nctx)
    right = jnp.logical_and(j != nctx - 1, j != nblk - 1)
    return left, right


def _mods_kernel(c_ref, w_ref, b_ref, o_ref):
    o_ref[...] = _mm(_silu(c_ref[...]), w_ref[...]) + b_ref[...]


def _mods(cond, ada_w, ada_b):
    depth, d, n = ada_w.shape
    r = cond.shape[0]
    nt = n // d
    return pl.pallas_call(
        _mods_kernel,
        grid=(depth, nt),
        in_specs=[pl.BlockSpec((r, d), lambda l, j: (0, 0)),
                  pl.BlockSpec((None, d, d), lambda l, j: (l, 0, j)),
                  pl.BlockSpec((None, 1, d), lambda l, j: (l, 0, j))],
        out_specs=pl.BlockSpec((None, r, d), lambda l, j: (l, 0, j)),
        out_shape=jax.ShapeDtypeStruct((depth, r, n), F32),
        name="adaln_mods",
    )(cond, ada_w, ada_b.reshape(depth, 1, n))


def _inproj_kernel(h_ref, mod_ref, nw_ref, w0, w1, w2, w3, o0, o1, o2, o3):
    xm = _norm_modulate(h_ref[...], nw_ref[...], mod_ref[0:1, :], mod_ref[1:2, :]).astype(BF16)
    for w, o in ((w0, o0), (w1, o1), (w2, o2), (w3, o3)):
        o[...] = jnp.dot(xm, w[...], preferred_element_type=F32)


def _token_grid_specs(bsz, tt, nctx):
    nt = tt // TOKEN_BLOCK
    tok = lambda width: pl.BlockSpec((None, TOKEN_BLOCK, width), lambda b, t: (b, t, 0))
    mod = pl.BlockSpec((None, 6, D_MODEL), lambda b, t: (2 * b + jnp.where(t >= nctx, 1, 0), 0, 0))
    return (bsz, nt), tok, mod


def _const_spec(shape):
    zeros = (0,) * len(shape)
    return pl.BlockSpec(shape, lambda b, t: zeros, pipeline_mode=pl.Buffered(1))


def _inproj(h, mods, norm_w, ws, nctx):
    bsz, tt, d = h.shape
    grid, tok, mod = _token_grid_specs(bsz, tt, nctx)
    return pl.pallas_call(
        _inproj_kernel,
        grid=grid,
        in_specs=[tok(d), mod, _const_spec((1, d))] + [_const_spec(w.shape) for w in ws],
        out_specs=[tok(w.shape[1]) for w in ws],
        out_shape=[jax.ShapeDtypeStruct((bsz, tt, w.shape[1]), F32) for w in ws],
        compiler_params=pltpu.CompilerParams(dimension_semantics=("parallel", "parallel"),
                                             vmem_limit_bytes=VMEM_LIMIT),
        name="in_proj",
    )(h, mods, norm_w.reshape(1, d), *ws)


def _scan_call(kernel, p, params, scratch, nctx, conv, name):
    bsz, tt, c = p.shape
    nblk = tt // TOKEN_BLOCK
    nhalo = tt // CONV_HALO
    per_halo = TOKEN_BLOCK // CONV_HALO
    fwd = lambda i: i
    rev = lambda i: _rev_block(i, nblk, nctx)

    def specs(blk):
        main = pl.BlockSpec((None, TOKEN_BLOCK, c), lambda b, i: (b, blk(i), 0))
        if not conv:
            return [main]
        prev = pl.BlockSpec((None, CONV_HALO, c),
                            lambda b, i: (b, jnp.maximum(blk(i) * per_halo - 1, 0), 0))
        nxt = pl.BlockSpec((None, CONV_HALO, c),
                           lambda b, i: (b, jnp.minimum((blk(i) + 1) * per_halo, nhalo - 1), 0))
        return [main, prev, nxt]

    n_p = 3 if conv else 1
    out = jax.ShapeDtypeStruct((bsz, tt, MIX_W), F32)
    return pl.pallas_call(
        functools.partial(kernel, nblk, nctx),
        grid=(bsz, nblk),
        in_specs=specs(fwd) + specs(rev) + [_const_spec(w.shape) for w in params],
        out_specs=[pl.BlockSpec((None, TOKEN_BLOCK, MIX_W), lambda b, i: (b, fwd(i), 0)),
                   pl.BlockSpec((None, TOKEN_BLOCK, MIX_W), lambda b, i: (b, rev(i), 0))],
        out_shape=[out, out],
        scratch_shapes=scratch,
        compiler_params=pltpu.CompilerParams(dimension_semantics=("parallel", "arbitrary"),
                                             vmem_limit_bytes=VMEM_LIMIT),
        name=name,
    )(*([p] * (2 * n_p)), *params)


def _ssd_kernel(nblk, nctx, pf, pfp, pfn, pb, pbp, pbn, cw, cb, dtb, nega, dskip,
                yf, yb, s_ref, xpad, xbc, cum_s, v_s, tmp):
    i = pl.program_id(1)
    tb = pf.shape[0]
    nch = tb // CHUNK
    hd = SSD_HEAD_DIM

    @pl.when(i == 0)
    def _():
        s_ref[...] = jnp.zeros_like(s_ref)

    for d, (p, pp, pn) in enumerate(((pf, pfp, pfn), (pb, pbp, pbn))):
        j = i if d == 0 else _rev_block(i, nblk, nctx)
        lv, rv = _conv_valid(j, nblk, nctx)
        _conv_silu(xbc, d, xpad, p, pp, pn, MIX_W, 2 * MIX_W, cw, cb, lv, rv)
        delta = _softplus(p[:, 3 * MIX_W:3 * MIX_W + LANES] + dtb[...])
        cum_s[d] = _mm_exact(_block_cumsum_matrix(tb, d), nega[...] * delta)
        for h in range(SSD_HEADS):
            lane = d * SSD_HEADS + h
            v_s[d, :, h * hd:(h + 1) * hd] = xbc[d, :, h * hd:(h + 1) * hd] * delta[:, lane:lane + 1]

    def chunk(c, carry):
        for d in (0, 1):
            rows = _chunk_rows(c, d, nch)
            y = yf if d == 0 else yb
            mask = _scan_mask(CHUNK, d)
            cum = cum_s[d, rows, :]
            cum_t = _rows_of(_one_hot_rows(SSD_HEADS, d * SSD_HEADS), cum)
            tot = cum[CHUNK - 1:CHUNK, :] if d == 0 else cum[0:1, :]
            e_cum, e_end, e_tot = jnp.exp(cum), jnp.exp(tot - cum), jnp.exp(tot)
            for g in range(SSD_GROUPS):
                bg = xbc[d, rows, MIX_W + g * SSD_STATE:MIX_W + (g + 1) * SSD_STATE]
                cg = xbc[d, rows, MIX_W + (SSD_GROUPS + g) * SSD_STATE:MIX_W + (SSD_GROUPS + g + 1) * SSD_STATE]
                scores = _mm_nt(cg, bg)
                inter = _mm(cg, s_ref[d, g])
                hpg = SSD_HEADS // SSD_GROUPS
                for jh in range(hpg):
                    h = g * hpg + jh
                    lane = d * SSD_HEADS + h
                    hs = slice(h * hd, (h + 1) * hd)
                    decay = jnp.exp(jnp.where(mask, cum[:, lane:lane + 1] - cum_t[h:h + 1, :], -jnp.inf))
                    vh = v_s[d, rows, hs]
                    yh = _mm(scores * decay, vh) + inter[:, jh * hd:(jh + 1) * hd] * e_cum[:, lane:lane + 1]
                    if d == 0:
                        yh = yh + xbc[d, rows, hs] * dskip[:, hs]
                    y[rows, hs] = yh
                    tmp[0:CHUNK, jh * hd:(jh + 1) * hd] = vh * e_end[:, lane:lane + 1]
                    tmp[CHUNK:CHUNK + 1, jh * hd:(jh + 1) * hd] = jnp.broadcast_to(e_tot[:, lane:lane + 1], (1, hd))
                s_ref[d, g] = s_ref[d, g] * tmp[CHUNK:CHUNK + 1, :] + _mm_tn(bg, tmp[0:CHUNK, :])
        return carry

    lax.fori_loop(0, nch, chunk, 0)


def _ssd_scan(p, cw, cb, dtb, nega, dskip, nctx):
    tb = TOKEN_BLOCK
    scratch = [pltpu.VMEM((2, SSD_GROUPS, SSD_STATE, MIX_W // SSD_GROUPS), F32),
               pltpu.VMEM((tb + 2 * CONV_HALO, LANES), F32),
               pltpu.VMEM((2, tb, 2 * MIX_W), F32),
               pltpu.VMEM((2, tb, LANES), F32),
               pltpu.VMEM((2, tb, MIX_W), F32),
               pltpu.VMEM((CHUNK + 8, MIX_W // SSD_GROUPS), F32)]
    return _scan_call(_ssd_kernel, p, (cw, cb, dtb, nega, dskip), scratch, nctx, True, "ssd_scan")


def _mlstm_kernel(nblk, nctx, pf, pb, ib, fb, yf, yb, c_ref, m_ref, b_s, li_s):
    i = pl.program_id(1)
    tb = pf.shape[0]
    nch = tb // CHUNK
    nh, dk, dv = MLSTM_HEADS, MLSTM_K, MLSTM_V
    kscale = dk ** -0.5
    q0, k0, v0, g0 = 0, nh * dk, 2 * nh * dk, 2 * nh * dk + 2 * MIX_W
    fl = 2 * nh

    @pl.when(i == 0)
    def _():
        c_ref[...] = jnp.zeros_like(c_ref)
        m_ref[...] = jnp.zeros_like(m_ref)

    for d, p in enumerate((pf, pb)):
        gates = p[:, g0:g0 + LANES]
        li_s[d] = pltpu.roll(gates + ib[...], fl, axis=1)
        b_s[d] = _mm_exact(_block_cumsum_matrix(tb, d), _log_sigmoid(gates + fb[...]))

    ones_col = jnp.where(_iota2((CHUNK, dv), 1) == 0, 1.0, 0.0).astype(F32)

    def chunk(c, carry):
        for d, p in enumerate((pf, pb)):
            rows = _chunk_rows(c, d, nch)
            y = yf if d == 0 else yb
            mask = _scan_mask(CHUNK, d)
            sel = _one_hot_rows(8, fl + d * nh)
            b = b_s[d, rows, :]
            li = li_s[d, rows, :]
            b_t = _rows_of(sel, b)
            li_t = _rows_of(sel, li)
            tot = b[CHUNK - 1:CHUNK, :] if d == 0 else b[0:1, :]
            m_row = m_ref[d:d + 1, :]
            inter = b + m_row
            end_log = tot - b + li
            m_new = jnp.maximum(tot + m_row, jnp.max(end_log, axis=0, keepdims=True))
            w_end = jnp.exp(end_log - m_new)
            keep = jnp.exp(tot + m_row - m_new)
            m_ref[d:d + 1, :] = m_new
            for h in range(nh):
                lane = fl + d * nh + h
                qh = p[rows, q0 + h * dk:q0 + (h + 1) * dk]
                kh = p[rows, k0 + h * dk:k0 + (h + 1) * dk] * kscale
                vext = jnp.concatenate([p[rows, v0 + h * dv:v0 + (h + 1) * dv], ones_col], axis=1)
                d_log = jnp.where(mask, b[:, lane:lane + 1] - b_t[h:h + 1, :] + li_t[h:h + 1, :], -jnp.inf)
                inter_h = inter[:, lane:lane + 1]
                m_t = jnp.maximum(inter_h, jnp.max(d_log, axis=1, keepdims=True))
                scores = _mm_nt(qh, kh) * jnp.exp(d_log - m_t)
                res = _mm(scores, vext) + _mm(qh, c_ref[d, h]) * jnp.exp(inter_h - m_t)
                den = jnp.maximum(jnp.abs(res[:, dv:dv + 1]), jnp.exp(-m_t))
                y[rows, h * dv:(h + 1) * dv] = res[:, 0:dv] / den
                c_ref[d, h] = c_ref[d, h] * keep[:, lane:lane + 1] + _mm_tn(kh * w_end[:, lane:lane + 1], vext)
        return carry

    lax.fori_loop(0, nch, chunk, 0)


def _mlstm_scan(p, ib, fb, nctx):
    tb = TOKEN_BLOCK
    scratch = [pltpu.VMEM((2, MLSTM_HEADS, MLSTM_K, 2 * MLSTM_V), F32),
               pltpu.VMEM((8, LANES), F32),
               pltpu.VMEM((2, tb, LANES), F32),
               pltpu.VMEM((2, tb, LANES), F32)]
    return _scan_call(_mlstm_kernel, p, (ib, fb), scratch, nctx, False, "mlstm_scan")


def _unit_triangular_inverse_minus_identity(a, lvl_ref):
    e = -(a * lvl_ref[0])
    for k in range(1, lvl_ref.shape[0]):
        ak = a * lvl_ref[k]
        x = ak + _mm(ak, e)
        e = e - (x + _mm(e, x))
    return e


def _gdn_kernel(nblk, nctx, pf, pfp, pfn, pb, pbp, pbn, cw, dtb, nega, yf, yb,
                s_ref, xpad, qkv, cum_s, beta_s, lvl):
    i = pl.program_id(1)
    tb = pf.shape[0]
    nch = tb // CHUNK
    nh, hd = GDN_HEADS, GDN_HEAD_DIM
    g0 = 4 * MIX_W

    @pl.when(i == 0)
    def _():
        s_ref[...] = jnp.zeros_like(s_ref)
        r, c = _iota2((CHUNK, CHUNK), 0), _iota2((CHUNK, CHUNK), 1)
        for k in range(lvl.shape[0]):
            joined = ((r >> (k + 1)) == (c >> (k + 1))) & (((r >> k) & 1) != ((c >> k) & 1))
            lvl[k] = jnp.where(joined, 1.0, 0.0).astype(F32)

    for d, (p, pp, pn) in enumerate(((pf, pfp, pfn), (pb, pbp, pbn))):
        j = i if d == 0 else _rev_block(i, nblk, nctx)
        lv, rv = _conv_valid(j, nblk, nctx)
        _conv_silu(qkv, d, xpad, p, pp, pn, 0, 3 * MIX_W, cw, None, lv, rv)
        for h in range(2 * nh):
            hs = slice(h * hd, (h + 1) * hd)
            x = qkv[d, :, hs]
            x = x * lax.rsqrt(jnp.sum(x * x, axis=-1, keepdims=True) + 1e-6)
            qkv[d, :, hs] = x * (hd ** -0.5) if h < nh else x
        gates = p[:, g0:g0 + LANES]
        cum_s[d] = _mm_exact(_block_cumsum_matrix(tb, d), nega[...] * _softplus(gates + dtb[...]))
        beta_s[d] = pltpu.roll(_sigmoid(gates), LANES - 2 * nh, axis=1)

    def chunk(c, carry):
        for d in (0, 1):
            rows = _chunk_rows(c, d, nch)
            y = yf if d == 0 else yb
            mask = _scan_mask(CHUNK, d)
            strict = _scan_mask(CHUNK, d, strict=True)
            cum = cum_s[d, rows, :]
            beta = beta_s[d, rows, :]
            cum_t = _rows_of(_one_hot_rows(8, d * nh), cum)
            tot = cum[CHUNK - 1:CHUNK, :] if d == 0 else cum[0:1, :]
            e_cum, e_end, e_tot = jnp.exp(cum), jnp.exp(tot - cum), jnp.exp(tot)
            for h in range(nh):
                lane = d * nh + h
                qh = qkv[d, rows, h * hd:(h + 1) * hd]
                kh = qkv[d, rows, MIX_W + h * hd:MIX_W + (h + 1) * hd]
                vh = qkv[d, rows, 2 * MIX_W + h * hd:2 * MIX_W + (h + 1) * hd]
                bcol = beta[:, lane:lane + 1]
                decay = jnp.exp(jnp.where(mask, cum[:, lane:lane + 1] - cum_t[h:h + 1, :], -jnp.inf))
                kb = kh * bcol
                a = jnp.where(strict, _mm_nt(kb, kh) * decay, 0.0)
                e = _unit_triangular_inverse_minus_identity(a, lvl)
                rhs = jnp.concatenate([kb * e_cum[:, lane:lane + 1], vh * bcol], axis=1)
                sol = rhs + _mm(e, rhs)
                s = s_ref[d, h]
                v_new = sol[:, hd:2 * hd] - _mm(sol[:, 0:hd], s)
                attn = _mm_nt(qh, kh) * decay
                y[rows, h * hd:(h + 1) * hd] = _mm(qh * e_cum[:, lane:lane + 1], s) + _mm(attn, v_new)
                s_ref[d, h] = s * e_tot[:, lane:lane + 1] + _mm_tn(kh * e_end[:, lane:lane + 1], v_new)
        return carry

    lax.fori_loop(0, nch, chunk, 0)


def _gdn_scan(p, cw, dtb, nega, nctx):
    tb = TOKEN_BLOCK
    nlevels = CHUNK.bit_length() - 1
    scratch = [pltpu.VMEM((2, GDN_HEADS, GDN_HEAD_DIM, GDN_HEAD_DIM), F32),
               pltpu.VMEM((tb + 2 * CONV_HALO, LANES), F32),
               pltpu.VMEM((2, tb, 3 * MIX_W), F32),
               pltpu.VMEM((2, tb, LANES), F32),
               pltpu.VMEM((2, tb, LANES), F32),
               pltpu.VMEM((nlevels, CHUNK, CHUNK), F32)]
    return _scan_call(_gdn_kernel, p, (cw, dtb, nega), scratch, nctx, True, "gdn_scan")


def _gla_kernel(nblk, nctx, pf, pb, up, gb, yf, yb, st_ref, cum_s):
    i = pl.program_id(1)
    tb = pf.shape[0]
    nch = tb // CHUNK
    nh, dk, dv = GLA_HEADS, GLA_K, GLA_V
    qscale = dk ** -0.5
    q0, k0, v0, g0 = 0, nh * dk, 2 * nh * dk, 2 * nh * dk + 2 * MIX_W
    mid = CHUNK // 2

    @pl.when(i == 0)
    def _():
        st_ref[...] = jnp.zeros_like(st_ref)

    for d, p in enumerate((pf, pb)):
        gk = _mm(p[:, g0:g0 + LANES], up[d]) + gb[d:d + 1, :]
        cum_s[d] = _mm_exact(_block_cumsum_matrix(tb, d), _log_sigmoid(gk) * (1.0 / GLA_GATE_NORM))

    def chunk(c, carry):
        for d, p in enumerate((pf, pb)):
            rows = _chunk_rows(c, d, nch)
            y = yf if d == 0 else yb
            mask = _scan_mask(CHUNK, d)
            cum = cum_s[d, rows, :]
            ref = cum[mid:mid + 1, :] if d == 0 else cum[CHUNK - 1 - mid:CHUNK - mid, :]
            tot = cum[CHUNK - 1:CHUNK, :] if d == 0 else cum[0:1, :]
            q = p[rows, q0:q0 + nh * dk] * qscale
            k = p[rows, k0:k0 + nh * dk]
            qg, kg = q * jnp.exp(cum - ref), k * jnp.exp(ref - cum)
            qe, ke = q * jnp.exp(cum), k * jnp.exp(tot - cum)
            e_tot = jnp.exp(tot)
            for h in range(nh):
                ks = slice(h * dk, (h + 1) * dk)
                vh = p[rows, v0 + h * dv:v0 + (h + 1) * dv]
                scores = jnp.where(mask, _mm_nt(qg[:, ks], kg[:, ks]), 0.0)
                st = st_ref[d, h]
                y[rows, h * dv:(h + 1) * dv] = _mm(scores, vh) + _mm_nt(qe[:, ks], st)
                st_ref[d, h] = st * e_tot[:, ks] + _mm_tn(vh, ke[:, ks])
        return carry

    lax.fori_loop(0, nch, chunk, 0)


def _gla_scan(p, up, gb, nctx):
    scratch = [pltpu.VMEM((2, GLA_HEADS, GLA_V, GLA_K), F32),
               pltpu.VMEM((2, TOKEN_BLOCK, GLA_HEADS * GLA_K), F32)]
    return _scan_call(_gla_kernel, p, (up, gb), scratch, nctx, False, "gla_scan")


def _merge_kernel(h_ref, mod_ref, nw_ref,
                  ssd_f, ssd_b, ml_f, ml_b, gd_f, gd_b, gl_f, gl_b,
                  z_ref, o_ref, gg_ref, lg_ref,
                  nssd, nml, ngd, ngl, wg_ref, wb_ref, wo_ref, out_ref):
    x = h_ref[...]
    xm = _norm_modulate(x, nw_ref[...], mod_ref[0:1, :], mod_ref[1:2, :]).astype(BF16)
    branches = (
        _group_rmsnorm((ssd_f[...] + ssd_b[...]) * _silu(z_ref[...]), nssd[...], SSD_GROUPS),
        _group_rmsnorm(ml_f[...] + ml_b[...], nml[...], MLSTM_HEADS) * _sigmoid(o_ref[...]),
        _group_rmsnorm(gd_f[...] + gd_b[...], ngd[...], GDN_HEADS) * _silu(gg_ref[...]),
        _group_rmsnorm(gl_f[...] + gl_b[...], ngl[...], GLA_HEADS) * _silu(lg_ref[...]),
    )
    u = None
    for n, br in enumerate(branches):
        gate = _sigmoid(jnp.dot(xm, wg_ref[:, n * D_MODEL:(n + 1) * D_MODEL], preferred_element_type=F32))
        term = gate * _mm(br, wb_ref[n])
        u = term if u is None else u + term
    out_ref[...] = x + mod_ref[2:3, :] * _mm(u, wo_ref[...])


def _merge(h, mods, norm_w, ys, p_ssd, p_ml, p_gd, p_gl, norms, wg, wb, wo, nctx):
    bsz, tt, d = h.shape
    grid, tok, mod = _token_grid_specs(bsz, tt, nctx)
    gate_spec = lambda blk: pl.BlockSpec((None, TOKEN_BLOCK, MIX_W), lambda b, t: (b, t, blk))
    return pl.pallas_call(
        _merge_kernel,
        grid=grid,
        in_specs=([tok(d), mod, _const_spec((1, d))] + [tok(MIX_W)] * 8
                  + [gate_spec(0), gate_spec(2), gate_spec(3), gate_spec(2)]
                  + [_const_spec((1, MIX_W))] * 4
                  + [_const_spec(wg.shape), _const_spec(wb.shape), _const_spec(wo.shape)]),
        out_specs=tok(d),
        out_shape=jax.ShapeDtypeStruct((bsz, tt, d), F32),
        compiler_params=pltpu.CompilerParams(dimension_semantics=("parallel", "parallel"),
                                             vmem_limit_bytes=VMEM_LIMIT),
        name="merge_out",
    )(h, mods, norm_w.reshape(1, d), *ys, p_ssd, p_ml, p_gd, p_gl,
      *[n.reshape(1, MIX_W) for n in norms], wg, wb, wo)


def _ffn_kernel(h_ref, mod_ref, nw_ref, wg_ref, wu_ref, wd_ref, out_ref):
    x = h_ref[...]
    xm = _norm_modulate(x, nw_ref[...], mod_ref[3:4, :], mod_ref[4:5, :]).astype(BF16)
    a = jnp.dot(xm, wg_ref[...], preferred_element_type=F32)
    b = jnp.dot(xm, wu_ref[...], preferred_element_type=F32)
    out_ref[...] = x + mod_ref[5:6, :] * _mm(_silu(a) * b, wd_ref[...])


def _ffn(h, mods, norm_w, wg, wu, wd, nctx):
    bsz, tt, d = h.shape
    grid, tok, mod = _token_grid_specs(bsz, tt, nctx)
    return pl.pallas_call(
        _ffn_kernel,
        grid=grid,
        in_specs=[tok(d), mod, _const_spec((1, d)), _const_spec(wg.shape), _const_spec(wu.shape),
                  _const_spec(wd.shape)],
        out_specs=tok(d),
        out_shape=jax.ShapeDtypeStruct((bsz, tt, d), F32),
        compiler_params=pltpu.CompilerParams(dimension_semantics=("parallel", "parallel"),
                                             vmem_limit_bytes=VMEM_LIMIT),
        name="swiglu",
    )(h, mods, norm_w.reshape(1, d), wg, wu, wd)


def _final_norm_kernel(h_ref, w_ref, o_ref):
    x = h_ref[...]
    o_ref[...] = x * lax.rsqrt(jnp.mean(x * x, axis=-1, keepdims=True) + 1e-6) * w_ref[...]


def _final_norm(h, w, ctx_len):
    bsz, tt, d = h.shape
    off = ctx_len // TOKEN_BLOCK
    nt = (tt - ctx_len) // TOKEN_BLOCK
    return pl.pallas_call(
        _final_norm_kernel,
        grid=(bsz, nt),
        in_specs=[pl.BlockSpec((None, TOKEN_BLOCK, d), lambda b, t: (b, t + off, 0)),
                  pl.BlockSpec((1, d), lambda b, t: (0, 0))],
        out_specs=pl.BlockSpec((None, TOKEN_BLOCK, d), lambda b, t: (b, t, 0)),
        out_shape=jax.ShapeDtypeStruct((bsz, tt - ctx_len, d), F32),
        name="final_norm",
    )(h, w.reshape(1, d))


def _pad_cols(w, total):
    return jnp.pad(w, ((0, 0), (0, total - w.shape[1])))


def _lane_row(values, lane0):
    v = values.reshape(-1).astype(F32)
    return jnp.zeros((1, LANES), F32).at[0, lane0:lane0 + v.shape[0]].set(v)


def _to_colmajor(x, ctx_len):
    bsz, tt, ch = x.shape
    lat = x[:, ctx_len:]
    rows = lat.shape[1] // GRID_W
    lat = lat.reshape(bsz, rows, GRID_W, ch).transpose(0, 2, 1, 3).reshape(bsz, rows * GRID_W, ch)
    return jnp.concatenate([x[:, :ctx_len], lat], axis=1)


def _from_colmajor(x, ctx_len):
    bsz, tt, ch = x.shape
    lat = x[:, ctx_len:]
    rows = lat.shape[1] // GRID_W
    lat = lat.reshape(bsz, GRID_W, rows, ch).transpose(0, 2, 1, 3).reshape(bsz, rows * GRID_W, ch)
    return jnp.concatenate([x[:, :ctx_len], lat], axis=1)


def kernel(x, c, ctx, c_ctx, ada_w, ada_b, norm_mix_w, w_in, ssd_conv_w, ssd_conv_b, ssd_dt_bias,
           ssd_a_log, ssd_d, ssd_norm_w, mlstm_i_bias, mlstm_f_bias, mlstm_norm_w, gdn_conv_w, gdn_a_log,
           gdn_dt_bias, gdn_norm_w, gla_gk_up, gla_gk_bias, gla_norm_w, w_branch, w_out, norm_ffn_w,
           ffn_w_gate, ffn_w_up, ffn_w_down, norm_f_w):
    bsz, seq, d = x.shape
    ctx_len = ctx.shape[1]
    depth = w_in.shape[0]
    assert d == D_MODEL and ctx_len % TOKEN_BLOCK == 0 and seq % TOKEN_BLOCK == 0 and seq % GRID_W == 0
    nctx = ctx_len // TOKEN_BLOCK

    cond = jnp.stack([jnp.broadcast_to(c_ctx, c.shape), c], axis=1).reshape(2 * bsz, d)
    mods_all = _mods(cond, ada_w, ada_b).reshape(depth, 2 * bsz, 6, d)

    ssd_cols = 2 * MIX_W + 2 * SSD_GROUPS * SSD_STATE + 2 * SSD_HEADS
    ml_cols = 2 * MLSTM_HEADS * MLSTM_K + 2 * MIX_W + 4 * MLSTM_HEADS
    gd_cols = 4 * MIX_W + 4 * GDN_HEADS
    gl_cols = 2 * GLA_HEADS * GLA_K + 2 * MIX_W + 2 * GLA_RANK
    o1, o2, o3, o4 = ssd_cols, ssd_cols + ml_cols, ssd_cols + ml_cols + gd_cols, ssd_cols + ml_cols + gd_cols + gl_cols

    h = jnp.concatenate([ctx, x], axis=1)
    for l in range(depth):
        wl = w_in[l]
        ws = (_pad_cols(wl[:, :o1], SSD_P).astype(BF16), _pad_cols(wl[:, o1:o2], MLSTM_P).astype(BF16),
              _pad_cols(wl[:, o2:o3], GDN_P).astype(BF16), _pad_cols(wl[:, o3:o4], GLA_P).astype(BF16))
        mods = mods_all[l]
        p_ssd, p_ml, p_gd, p_gl = _inproj(h, mods, norm_mix_w[l], ws, nctx)

        ssd_y = _ssd_scan(p_ssd, ssd_conv_w[l], ssd_conv_b[l].reshape(1, -1),
                          _lane_row(ssd_dt_bias[l], 0), _lane_row(-jnp.exp(ssd_a_log[l]), 0),
                          jnp.repeat(ssd_d[l], SSD_HEAD_DIM).reshape(1, MIX_W), nctx)
        ml_y = _mlstm_scan(_to_colmajor(p_ml, ctx_len), _lane_row(mlstm_i_bias[l], 0),
                           _lane_row(mlstm_f_bias[l], 2 * MLSTM_HEADS), nctx)
        gd_y = _gdn_scan(_to_colmajor(p_gd, ctx_len), gdn_conv_w[l],
                         _lane_row(gdn_dt_bias[l], 0), _lane_row(-jnp.exp(gdn_a_log[l]), 0), nctx)
        up = jnp.zeros((2, LANES, GLA_HEADS * GLA_K), F32)
        for dd in range(2):
            up = up.at[dd, dd * GLA_RANK:(dd + 1) * GLA_RANK].set(gla_gk_up[l, dd])
        gl_y = _gla_scan(p_gl, up.astype(BF16), gla_gk_bias[l], nctx)

        ys = (ssd_y[0], ssd_y[1], _from_colmajor(ml_y[0], ctx_len), _from_colmajor(ml_y[1], ctx_len),
              _from_colmajor(gd_y[0], ctx_len), _from_colmajor(gd_y[1], ctx_len), gl_y[0], gl_y[1])
        h = _merge(h, mods, norm_mix_w[l], ys, p_ssd, p_ml, p_gd, p_gl,
                   (ssd_norm_w[l], mlstm_norm_w[l], gdn_norm_w[l], gla_norm_w[l]),
                   wl[:, o4:].astype(BF16), w_branch[l].astype(BF16), w_out[l].astype(BF16), nctx)
        h = _ffn(h, mods, norm_ffn_w[l], ffn_w_gate[l].astype(BF16), ffn_w_up[l].astype(BF16),
                 ffn_w_down[l].astype(BF16), nctx)
    return _final_norm(h, norm_f_w, ctx_len)
```

```python
import functools

import jax
import jax.numpy as jnp
from jax import lax
from jax.experimental import pallas as pl
from jax.experimental.pallas import tpu as pltpu

F32 = jnp.float32
BF16 = jnp.bfloat16
HIGHEST = lax.Precision.HIGHEST

D_MODEL = 1024
GRID_W = 64
CHUNK = 64
CHUNK_SHIFT = CHUNK.bit_length() - 1
CONV_K = 5
CONV_HALO = 8
MIX_W = 512
LANES = 128
TOKEN_BLOCK = 256
FFN_HIDDEN = 2816
VMEM_LIMIT = 56 * 1024 * 1024

SSD_HEADS, SSD_HEAD_DIM, SSD_STATE, SSD_GROUPS = 8, 64, 128, 2
MLSTM_HEADS, MLSTM_K, MLSTM_V = 4, 64, 128
GDN_HEADS, GDN_HEAD_DIM = 4, 128
GLA_HEADS, GLA_K, GLA_V, GLA_RANK, GLA_GATE_NORM = 4, 64, 128, 16, 16

SSD_P = 2 * MIX_W + 2 * SSD_GROUPS * SSD_STATE + LANES
MLSTM_P = 2 * MLSTM_HEADS * MLSTM_K + 2 * MIX_W + LANES
GDN_P = 4 * MIX_W + LANES
GLA_P = 2 * GLA_HEADS * GLA_K + 2 * MIX_W + LANES


def _mm(a, b):
    return jnp.dot(a.astype(BF16), b.astype(BF16), preferred_element_type=F32)


def _mm_nt(a, b):
    return lax.dot_general(a.astype(BF16), b.astype(BF16), (((1,), (1,)), ((), ())),
                           preferred_element_type=F32)


def _mm_tn(a, b):
    return lax.dot_general(a.astype(BF16), b.astype(BF16), (((0,), (0,)), ((), ())),
                           preferred_element_type=F32)


def _mm_exact(a, b):
    return jnp.dot(a, b, precision=HIGHEST, preferred_element_type=F32)


def _rows_of(sel, x):
    return lax.dot_general(sel, x, (((1,), (1,)), ((), ())), precision=HIGHEST,
                           preferred_element_type=F32)


def _sigmoid(x):
    return 1.0 / (1.0 + jnp.exp(-x))


def _silu(x):
    return x * _sigmoid(x)


def _softplus(x):
    return jnp.maximum(x, 0.0) + jnp.log1p(jnp.exp(-jnp.abs(x)))


def _log_sigmoid(x):
    return jnp.minimum(x, 0.0) - jnp.log1p(jnp.exp(-jnp.abs(x)))


def _iota2(shape, axis):
    return lax.broadcasted_iota(jnp.int32, shape, axis)


def _block_cumsum_matrix(n, d):
    r, c = _iota2((n, n), 0), _iota2((n, n), 1)
    same = (r >> CHUNK_SHIFT) == (c >> CHUNK_SHIFT)
    tri = (c <= r) if d == 0 else (c >= r)
    return jnp.where(same & tri, 1.0, 0.0).astype(F32)


def _one_hot_rows(nrows, lane0):
    r, c = _iota2((nrows, LANES), 0), _iota2((nrows, LANES), 1)
    return jnp.where(c == r + lane0, 1.0, 0.0).astype(F32)


def _rev_block(i, nblk, nctx):
    return jnp.where(i < nctx, nctx - 1 - i, nblk - 1 - (i - nctx))


def _chunk_rows(c, d, nch):
    r0 = c * CHUNK if d == 0 else (nch - 1 - c) * CHUNK
    return pl.ds(pl.multiple_of(r0, CHUNK), CHUNK)


def _group_rmsnorm(x, w, groups, eps=1e-6):
    gw = x.shape[-1] // groups
    outs = []
    for g in range(groups):
        xg = x[:, g * gw:(g + 1) * gw]
        ms = jnp.mean(xg * xg, axis=-1, keepdims=True)
        outs.append(xg * lax.rsqrt(ms + eps))
    return jnp.concatenate(outs, axis=-1) * w


def _norm_modulate(x, nw, shift, scale, eps=1e-6):
    ms = jnp.mean(x * x, axis=-1, keepdims=True)
    y = x * lax.rsqrt(ms + eps) * nw
    return y * (1.0 + scale) + shift


def _conv_silu(dst_ref, d, xpad_ref, p_ref, prev_ref, next_ref, col0, width, w_ref, b_ref,
               left_valid, right_valid):
    tb = p_ref.shape[0]
    for s in range(width // LANES):
        cols = slice(col0 + s * LANES, col0 + (s + 1) * LANES)
        ocols = slice(s * LANES, (s + 1) * LANES)
        xpad_ref[0:CONV_HALO, :] = jnp.where(left_valid, prev_ref[:, cols], 0.0)
        xpad_ref[CONV_HALO:CONV_HALO + tb, :] = p_ref[:, cols]
        xpad_ref[CONV_HALO + tb:2 * CONV_HALO + tb, :] = jnp.where(right_valid, next_ref[:, cols], 0.0)
        base = CONV_HALO - CONV_K // 2
        acc = w_ref[0:1, ocols] * xpad_ref[base:base + tb, :]
        for k in range(1, CONV_K):
            acc = acc + w_ref[k:k + 1, ocols] * xpad_ref[base + k:base + k + tb, :]
        if b_ref is not None:
            acc = acc + b_ref[:, ocols]
        dst_ref[d, :, ocols] = _silu(acc)


def _conv_valid(j, nblk, nctx):
    left = jnp.logical_and(j != 0, j != nctx)
    right = jnp.logical_and(j != nctx - 1, j != nblk - 1)
    return left, right


def _mods_kernel(c_ref, w_ref, b_ref, o_ref):
    o_ref[...] = _mm(_silu(c_ref[...]), w_ref[...]) + b_ref[...]


def _mods(cond, ada_w, ada_b):
    depth, d, n = ada_w.shape
    r = cond.shape[0]
    nt = n // d
    return pl.pallas_call(
        _mods_kernel,
        grid=(depth, nt),
        in_specs=[pl.BlockSpec((r, d), lambda l, j: (0, 0)),
                  pl.BlockSpec((None, d, d), lambda l, j: (l, 0, j)),
                  pl.BlockSpec((None, 1, d), lambda l, j: (l, 0, j))],
        out_specs=pl.BlockSpec((None, r, d), lambda l, j: (l, 0, j)),
        out_shape=jax.ShapeDtypeStruct((depth, r, n), F32),
        name="adaln_mods",
    )(cond, ada_w, ada_b.reshape(depth, 1, n))


def _inproj_kernel(h_ref, mod_ref, nw_ref, w0, w1, w2, w3, o0, o1, o2, o3):
    xm = _norm_modulate(h_ref[...], nw_ref[...], mod_ref[0:1, :], mod_ref[1:2, :]).astype(BF16)
    for w, o in ((w0, o0), (w1, o1), (w2, o2), (w3, o3)):
        o[...] = jnp.dot(xm, w[...], preferred_element_type=F32)


def _token_grid_specs(bsz, tt, nctx):
    nt = tt // TOKEN_BLOCK
    tok = lambda width: pl.BlockSpec((None, TOKEN_BLOCK, width), lambda b, t: (b, t, 0))
    mod = pl.BlockSpec((None, 6, D_MODEL), lambda b, t: (2 * b + jnp.where(t >= nctx, 1, 0), 0, 0))
    return (bsz, nt), tok, mod


def _const_spec(shape):
    zeros = (0,) * len(shape)
    return pl.BlockSpec(shape, lambda b, t: zeros, pipeline_mode=pl.Buffered(1))


def _inproj(h, mods, norm_w, ws, nctx):
    bsz, tt, d = h.shape
    grid, tok, mod = _token_grid_specs(bsz, tt, nctx)
    return pl.pallas_call(
        _inproj_kernel,
        grid=grid,
        in_specs=[tok(d), mod, _const_spec((1, d))] + [_const_spec(w.shape) for w in ws],
        out_specs=[tok(w.shape[1]) for w in ws],
        out_shape=[jax.ShapeDtypeStruct((bsz, tt, w.shape[1]), F32) for w in ws],
        compiler_params=pltpu.CompilerParams(dimension_semantics=("parallel", "parallel"),
                                             vmem_limit_bytes=VMEM_LIMIT),
        name="in_proj",
    )(h, mods, norm_w.reshape(1, d), *ws)


def _scan_call(kernel, p, params, scratch, nctx, conv, name):
    bsz, tt, c = p.shape
    nblk = tt // TOKEN_BLOCK
    nhalo = tt // CONV_HALO
    per_halo = TOKEN_BLOCK // CONV_HALO
    fwd = lambda i: i
    rev = lambda i: _rev_block(i, nblk, nctx)

    def specs(blk):
        main = pl.BlockSpec((None, TOKEN_BLOCK, c), lambda b, i: (b, blk(i), 0))
        if not conv:
            return [main]
        prev = pl.BlockSpec((None, CONV_HALO, c),
                            lambda b, i: (b, jnp.maximum(blk(i) * per_halo - 1, 0), 0))
        nxt = pl.BlockSpec((None, CONV_HALO, c),
                           lambda b, i: (b, jnp.minimum((blk(i) + 1) * per_halo, nhalo - 1), 0))
        return [main, prev, nxt]

    n_p = 3 if conv else 1
    out = jax.ShapeDtypeStruct((bsz, tt, MIX_W), F32)
    return pl.pallas_call(
        functools.partial(kernel, nblk, nctx),
        grid=(bsz, nblk),
        in_specs=specs(fwd) + specs(rev) + [_const_spec(w.shape) for w in params],
        out_specs=[pl.BlockSpec((None, TOKEN_BLOCK, MIX_W), lambda b, i: (b, fwd(i), 0)),
                   pl.BlockSpec((None, TOKEN_BLOCK, MIX_W), lambda b, i: (b, rev(i), 0))],
        out_shape=[out, out],
        scratch_shapes=scratch,
        compiler_params=pltpu.CompilerParams(dimension_semantics=("parallel", "arbitrary"),
                                             vmem_limit_bytes=VMEM_LIMIT),
        name=name,
    )(*([p] * (2 * n_p)), *params)


def _ssd_kernel(nblk, nctx, pf, pfp, pfn, pb, pbp, pbn, cw, cb, dtb, nega, dskip,
                yf, yb, s_ref, xpad, xbc, cum_s, v_s):
    i = pl.program_id(1)
    tb = pf.shape[0]
    nch = tb // CHUNK
    hd = SSD_HEAD_DIM
    assert hd == CHUNK

    @pl.when(i == 0)
    def _():
        s_ref[...] = jnp.zeros_like(s_ref)

    for d, (p, pp, pn) in enumerate(((pf, pfp, pfn), (pb, pbp, pbn))):
        j = i if d == 0 else _rev_block(i, nblk, nctx)
        lv, rv = _conv_valid(j, nblk, nctx)
        _conv_silu(xbc, d, xpad, p, pp, pn, MIX_W, 2 * MIX_W, cw, cb, lv, rv)
        delta = _softplus(p[:, 3 * MIX_W:3 * MIX_W + LANES] + dtb[...])
        cum_s[d] = _mm_exact(_block_cumsum_matrix(tb, d), nega[...] * delta)
        for h in range(SSD_HEADS):
            lane = d * SSD_HEADS + h
            v_s[d, :, h * hd:(h + 1) * hd] = xbc[d, :, h * hd:(h + 1) * hd] * delta[:, lane:lane + 1]

    hpg = SSD_HEADS // SSD_GROUPS
    gw = hpg * hd
    groups = [(d, g) for d in (0, 1) for g in range(SSD_GROUPS)]
    blk = _iota2((CHUNK, gw), 1) >> CHUNK_SHIFT
    col = _iota2((CHUNK, gw), 1) & (CHUNK - 1)
    row = _iota2((CHUNK, gw), 0)

    def group_cols(x, base):
        out = x[:, base:base + 1]
        for j in range(1, hpg):
            out = jnp.where(blk[0:x.shape[0]] == j, x[:, base + j:base + j + 1], out)
        return out

    def chunk(c, carry):
        dirs = []
        for d in (0, 1):
            rows = _chunk_rows(c, d, nch)
            cum = cum_s[d, rows, :]
            tot = cum[CHUNK - 1:CHUNK, :] if d == 0 else cum[0:1, :]
            cum_t = _rows_of(_one_hot_rows(SSD_HEADS, d * SSD_HEADS), jnp.concatenate([cum] * hpg, axis=0))
            dirs.append(dict(rows=rows, cum=cum, tot=tot, cum_t=cum_t))

        st = []
        for d, g in groups:
            dd = dirs[d]
            rows = dd["rows"]
            base = d * SSD_HEADS + g * hpg
            incl = (col <= row) if d == 0 else (col >= row)
            ccol = group_cols(dd["cum"], base)
            crow = dd["cum_t"][g * hpg:g * hpg + 1, :]
            for j in range(1, hpg):
                crow = jnp.where(blk[0:1] == j, dd["cum_t"][g * hpg + j:g * hpg + j + 1, :], crow)
            tot = group_cols(dd["tot"], base)
            decay = jnp.exp(jnp.where(incl, ccol - crow, -jnp.inf))
            bg = xbc[d, rows, MIX_W + g * SSD_STATE:MIX_W + (g + 1) * SSD_STATE]
            cg = xbc[d, rows, MIX_W + (SSD_GROUPS + g) * SSD_STATE:MIX_W + (SSD_GROUPS + g + 1) * SSD_STATE]
            vg = v_s[d, rows, g * gw:(g + 1) * gw]
            scores = _mm_nt(cg, jnp.concatenate([bg] * hpg, axis=0))
            inter = _mm(cg, s_ref[d, g])
            v_bd = jnp.concatenate([jnp.where(blk == j, vg, 0.0) for j in range(hpg)], axis=0)
            st.append(dict(p=scores * decay, inter=inter * jnp.exp(ccol), v_bd=v_bd, bg=bg,
                           v_end=vg * jnp.exp(tot - ccol), e_tot=jnp.exp(tot)))

        for (d, g), s in zip(groups, st):
            rows = dirs[d]["rows"]
            y = yf if d == 0 else yb
            yg = _mm(s["p"], s["v_bd"]) + s["inter"]
            if d == 0:
                yg = yg + xbc[d, rows, g * gw:(g + 1) * gw] * dskip[:, g * gw:(g + 1) * gw]
            y[rows, g * gw:(g + 1) * gw] = yg
            s_ref[d, g] = s_ref[d, g] * s["e_tot"] + _mm_tn(s["bg"], s["v_end"])
        return carry

    lax.fori_loop(0, nch, chunk, 0)


def _ssd_scan(p, cw, cb, dtb, nega, dskip, nctx):
    tb = TOKEN_BLOCK
    scratch = [pltpu.VMEM((2, SSD_GROUPS, SSD_STATE, MIX_W // SSD_GROUPS), F32),
               pltpu.VMEM((tb + 2 * CONV_HALO, LANES), F32),
               pltpu.VMEM((2, tb, 2 * MIX_W), F32),
               pltpu.VMEM((2, tb, LANES), F32),
               pltpu.VMEM((2, tb, MIX_W), F32)]
    return _scan_call(_ssd_kernel, p, (cw, cb, dtb, nega, dskip), scratch, nctx, True, "ssd_scan")


def _mlstm_kernel(nblk, nctx, pf, pb, ib, fb, yf, yb, c_ref, m_ref, b_s, li_s):
    i = pl.program_id(1)
    tb = pf.shape[0]
    nch = tb // CHUNK
    nh, dk, dv = MLSTM_HEADS, MLSTM_K, MLSTM_V
    kscale = dk ** -0.5
    q0, k0, v0, g0 = 0, nh * dk, 2 * nh * dk, 2 * nh * dk + 2 * MIX_W
    fl = 2 * nh

    @pl.when(i == 0)
    def _():
        c_ref[...] = jnp.zeros_like(c_ref)
        m_ref[...] = jnp.zeros_like(m_ref)

    for d, p in enumerate((pf, pb)):
        gates = p[:, g0:g0 + LANES]
        li_s[d] = pltpu.roll(gates + ib[...], fl, axis=1)
        b_s[d] = _mm_exact(_block_cumsum_matrix(tb, d), _log_sigmoid(gates + fb[...]))

    ones_col = jnp.where(_iota2((CHUNK, dv), 1) == 0, 1.0, 0.0).astype(F32)
    npair = nh // 2
    pairs = [(d, hp) for d in (0, 1) for hp in range(npair)]
    lane_hi = _iota2((CHUNK, 2 * CHUNK), 1) >= CHUNK
    col = _iota2((CHUNK, 2 * CHUNK), 1) & (CHUNK - 1)
    row = _iota2((CHUNK, 2 * CHUNK), 0)
    ew = 2 * dv

    def chunk(c, carry):
        dirs = []
        for d in (0, 1):
            rows = _chunk_rows(c, d, nch)
            sel = _one_hot_rows(8, fl + d * nh)
            b = b_s[d, rows, :]
            li = li_s[d, rows, :]
            tot = b[CHUNK - 1:CHUNK, :] if d == 0 else b[0:1, :]
            m_row = m_ref[d:d + 1, :]
            end_log = tot - b + li
            m_new = jnp.maximum(tot + m_row, jnp.max(end_log, axis=0, keepdims=True))
            m_ref[d:d + 1, :] = m_new
            dirs.append(dict(rows=rows, b=b, inter=b + m_row,
                             b_t=_rows_of(sel, jnp.concatenate([b, b], axis=0)),
                             li_t=_rows_of(sel, jnp.concatenate([li, li], axis=0)),
                             w_end=jnp.exp(end_log - m_new), keep=jnp.exp(tot + m_row - m_new)))

        st = []
        for d, hp in pairs:
            g = dirs[d]
            p = pf if d == 0 else pb
            la = fl + d * nh + 2 * hp
            incl = (col <= row) if d == 0 else (col >= row)
            pick = lambda t: jnp.where(lane_hi[0:1], t[2 * hp + 1:2 * hp + 2, :], t[2 * hp:2 * hp + 1, :])
            d_log = jnp.where(incl, _pair_cols(g["b"], lane_hi, la) - pick(g["b_t"]) + pick(g["li_t"]), -jnp.inf)
            m_t = [jnp.maximum(g["inter"][:, la:la + 1],
                               jnp.max(jnp.where(lane_hi, -jnp.inf, d_log), axis=1, keepdims=True)),
                   jnp.maximum(g["inter"][:, la + 1:la + 2],
                               jnp.max(jnp.where(lane_hi, d_log, -jnp.inf), axis=1, keepdims=True))]
            m_pair = jnp.where(lane_hi, m_t[1], m_t[0])
            q = p[g["rows"], q0 + hp * 2 * dk:q0 + (hp + 1) * 2 * dk]
            k = p[g["rows"], k0 + hp * 2 * dk:k0 + (hp + 1) * 2 * dk] * kscale
            vext = [jnp.concatenate([p[g["rows"], v0 + (2 * hp + h) * dv:v0 + (2 * hp + h + 1) * dv], ones_col],
                                    axis=1) for h in (0, 1)]
            scores = _mm_nt(q, _block_diag2(k, lane_hi)) * jnp.exp(d_log - m_pair)
            qw = q * jnp.exp(_pair_cols(g["inter"], lane_hi, la) - m_pair)
            st.append(dict(lhs=jnp.concatenate([scores, qw], axis=1), m_t=m_t, k=k, vext=vext))

        res = [_mm(s["lhs"], jnp.concatenate([_block_diag_wide(s["vext"][0], s["vext"][1]), c_ref[d, hp]], axis=0))
               for (d, hp), s in zip(pairs, st)]

        for (d, hp), s, r in zip(pairs, st, res):
            g = dirs[d]
            la = fl + d * nh + 2 * hp
            y = yf if d == 0 else yb
            kw = s["k"] * _pair_cols(g["w_end"], lane_hi, la)
            for h in (0, 1):
                den = jnp.maximum(jnp.abs(r[:, h * ew + dv:h * ew + dv + 1]), jnp.exp(-s["m_t"][h]))
                y[g["rows"], (2 * hp + h) * dv:(2 * hp + h + 1) * dv] = r[:, h * ew:h * ew + dv] / den
                kw_h = jnp.where(lane_hi, kw, 0.0) if h == 1 else jnp.where(lane_hi, 0.0, kw)
                cs = slice(h * ew, (h + 1) * ew)
                c_ref[d, hp, :, cs] = c_ref[d, hp, :, cs] * g["keep"][:, la + h:la + h + 1] + _mm_tn(kw_h, s["vext"][h])
        return carry

    lax.fori_loop(0, nch, chunk, 0)


def _mlstm_scan(p, ib, fb, nctx):
    tb = TOKEN_BLOCK
    scratch = [pltpu.VMEM((2, MLSTM_HEADS // 2, 2 * MLSTM_K, 4 * MLSTM_V), F32),
               pltpu.VMEM((8, LANES), F32),
               pltpu.VMEM((2, tb, LANES), F32),
               pltpu.VMEM((2, tb, LANES), F32)]
    return _scan_call(_mlstm_kernel, p, (ib, fb), scratch, nctx, False, "mlstm_scan")


def _pair_cols(x, lane_hi, la):
    return jnp.where(lane_hi, x[:, la + 1:la + 2], x[:, la:la + 1])


def _block_diag2(x, lane_hi):
    return jnp.concatenate([jnp.where(lane_hi, 0.0, x), jnp.where(lane_hi, x, 0.0)], axis=0)


def _block_diag_wide(a, b):
    z = jnp.zeros_like(a)
    return jnp.concatenate([jnp.concatenate([a, z], axis=1), jnp.concatenate([z, b], axis=1)], axis=0)


def _gdn_kernel(nblk, nctx, pf, pfp, pfn, pb, pbp, pbn, cw, dtb, nega, yf, yb,
                s_ref, xpad, qkv, cum_s, beta_s, lvl):
    i = pl.program_id(1)
    tb = pf.shape[0]
    nch = tb // CHUNK
    nh, hd = GDN_HEADS, GDN_HEAD_DIM
    g0 = 4 * MIX_W
    npair = nh // 2
    pw = 2 * hd

    @pl.when(i == 0)
    def _():
        s_ref[...] = jnp.zeros_like(s_ref)
        r, c = _iota2((CHUNK, 2 * CHUNK), 0), _iota2((CHUNK, 2 * CHUNK), 1) & (CHUNK - 1)
        for k in range(lvl.shape[0]):
            joined = ((r >> (k + 1)) == (c >> (k + 1))) & (((r >> k) & 1) != ((c >> k) & 1))
            lvl[k] = jnp.where(joined, 1.0, 0.0).astype(F32)

    for d, (p, pp, pn) in enumerate(((pf, pfp, pfn), (pb, pbp, pbn))):
        j = i if d == 0 else _rev_block(i, nblk, nctx)
        lv, rv = _conv_valid(j, nblk, nctx)
        _conv_silu(qkv, d, xpad, p, pp, pn, 0, 3 * MIX_W, cw, None, lv, rv)
        for h in range(2 * nh):
            hs = slice(h * hd, (h + 1) * hd)
            x = qkv[d, :, hs]
            x = x * lax.rsqrt(jnp.sum(x * x, axis=-1, keepdims=True) + 1e-6)
            qkv[d, :, hs] = x * (hd ** -0.5) if h < nh else x
        gates = p[:, g0:g0 + LANES]
        cum_s[d] = _mm_exact(_block_cumsum_matrix(tb, d), nega[...] * _softplus(gates + dtb[...]))
        beta_s[d] = pltpu.roll(_sigmoid(gates), LANES - 2 * nh, axis=1)

    pairs = [(d, hp) for d in (0, 1) for hp in range(npair)]
    lane_hi = _iota2((CHUNK, 2 * CHUNK), 1) >= CHUNK
    col = _iota2((CHUNK, 2 * CHUNK), 1) & (CHUNK - 1)
    row = _iota2((CHUNK, 2 * CHUNK), 0)

    def chunk(c, carry):
        dirs = []
        for d in (0, 1):
            rows = _chunk_rows(c, d, nch)
            cum = cum_s[d, rows, :]
            cum2 = jnp.concatenate([cum, cum], axis=0)
            tot = cum[CHUNK - 1:CHUNK, :] if d == 0 else cum[0:1, :]
            dirs.append(dict(rows=rows, cum=cum, beta=beta_s[d, rows, :],
                             cum_t=_rows_of(_one_hot_rows(8, d * nh), cum2),
                             e_cum=jnp.exp(cum), e_end=jnp.exp(tot - cum), e_tot=jnp.exp(tot)))

        st = []
        for d, hp in pairs:
            g = dirs[d]
            la = d * nh + 2 * hp
            incl = (col <= row) if d == 0 else (col >= row)
            strict = (col < row) if d == 0 else (col > row)
            crow = jnp.where(lane_hi[0:1], g["cum_t"][2 * hp + 1:2 * hp + 2, :], g["cum_t"][2 * hp:2 * hp + 1, :])
            decay = jnp.exp(jnp.where(incl, _pair_cols(g["cum"], lane_hi, la) - crow, -jnp.inf))
            sl = lambda base, h: qkv[d, g["rows"], base + (2 * hp + h) * hd:base + (2 * hp + h + 1) * hd]
            q = [sl(0, 0), sl(0, 1)]
            k = [sl(MIX_W, 0), sl(MIX_W, 1)]
            v = [sl(2 * MIX_W, 0), sl(2 * MIX_W, 1)]
            bcol = [g["beta"][:, la + h:la + h + 1] for h in (0, 1)]
            ecol = [g["e_cum"][:, la + h:la + h + 1] for h in (0, 1)]
            kb = [k[h] * bcol[h] for h in (0, 1)]
            lhs = jnp.concatenate([jnp.concatenate(kb, axis=1), jnp.concatenate(q, axis=1)], axis=0)
            prod = _mm_nt(lhs, _block_diag_wide(k[0], k[1]))
            st.append(dict(a=jnp.where(strict, prod[0:CHUNK] * decay, 0.0), attn=prod[CHUNK:2 * CHUNK] * decay,
                           k=k, kb=kb, v=v, bcol=bcol, ecol=ecol,
                           qe=jnp.concatenate([q[h] * ecol[h] for h in (0, 1)], axis=1)))

        es = [-(s["a"] * lvl[0]) for s in st]
        for lv in range(1, lvl.shape[0]):
            aks = [s["a"] * lvl[lv] for s in st]
            xs = [ak + _mm(ak, _block_diag2(e, lane_hi)) for ak, e in zip(aks, es)]
            es = [e - (x + _mm(e, _block_diag2(x, lane_hi))) for e, x in zip(es, xs)]

        sols = []
        for s, e in zip(st, es):
            rhs = [jnp.concatenate([s["kb"][h] * s["ecol"][h], s["v"][h] * s["bcol"][h]], axis=1) for h in (0, 1)]
            sols.append(jnp.concatenate(rhs, axis=1) + _mm(e, _block_diag_wide(rhs[0], rhs[1])))

        rs = []
        for (d, hp), s, sol in zip(pairs, st, sols):
            w = jnp.concatenate([sol[:, 0:hd], sol[:, 2 * hd:3 * hd]], axis=1)
            rs.append(_mm(jnp.concatenate([w, s["qe"]], axis=0), s_ref[d, hp]))

        for (d, hp), s, sol, r in zip(pairs, st, sols, rs):
            g = dirs[d]
            la = d * nh + 2 * hp
            y = yf if d == 0 else yb
            v_new = [sol[:, hd:2 * hd] - r[0:CHUNK, 0:hd], sol[:, 3 * hd:4 * hd] - r[0:CHUNK, hd:pw]]
            y[g["rows"], hp * pw:(hp + 1) * pw] = (r[CHUNK:2 * CHUNK]
                                                  + _mm(s["attn"], _block_diag_wide(v_new[0], v_new[1])))
            for h in (0, 1):
                blk = slice(h * hd, (h + 1) * hd)
                k_end = s["k"][h] * g["e_end"][:, la + h:la + h + 1]
                s_ref[d, hp, blk, blk] = (s_ref[d, hp, blk, blk] * g["e_tot"][:, la + h:la + h + 1]
                                          + _mm_tn(k_end, v_new[h]))
        return carry

    lax.fori_loop(0, nch, chunk, 0)


def _gdn_scan(p, cw, dtb, nega, nctx):
    tb = TOKEN_BLOCK
    nlevels = CHUNK.bit_length() - 1
    pw = 2 * GDN_HEAD_DIM
    scratch = [pltpu.VMEM((2, GDN_HEADS // 2, pw, pw), F32),
               pltpu.VMEM((tb + 2 * CONV_HALO, LANES), F32),
               pltpu.VMEM((2, tb, 3 * MIX_W), F32),
               pltpu.VMEM((2, tb, LANES), F32),
               pltpu.VMEM((2, tb, LANES), F32),
               pltpu.VMEM((nlevels, CHUNK, 2 * CHUNK), F32)]
    return _scan_call(_gdn_kernel, p, (cw, dtb, nega), scratch, nctx, True, "gdn_scan")


def _gla_kernel(nblk, nctx, pf, pb, up, gb, yf, yb, st_ref, cum_s):
    i = pl.program_id(1)
    tb = pf.shape[0]
    nch = tb // CHUNK
    nh, dk, dv = GLA_HEADS, GLA_K, GLA_V
    qscale = dk ** -0.5
    q0, k0, v0, g0 = 0, nh * dk, 2 * nh * dk, 2 * nh * dk + 2 * MIX_W
    mid = CHUNK // 2

    @pl.when(i == 0)
    def _():
        st_ref[...] = jnp.zeros_like(st_ref)

    for d, p in enumerate((pf, pb)):
        gk = _mm(p[:, g0:g0 + LANES], up[d]) + gb[d:d + 1, :]
        cum_s[d] = _mm_exact(_block_cumsum_matrix(tb, d), _log_sigmoid(gk) * (1.0 / GLA_GATE_NORM))

    npair = nh // 2
    pairs = [(d, hp) for d in (0, 1) for hp in range(npair)]
    lane_hi = _iota2((CHUNK, 2 * CHUNK), 1) >= CHUNK
    col = _iota2((CHUNK, 2 * CHUNK), 1) & (CHUNK - 1)
    row = _iota2((CHUNK, 2 * CHUNK), 0)
    assert dk == CHUNK

    def chunk(c, carry):
        st = []
        for d, hp in pairs:
            p = pf if d == 0 else pb
            rows = _chunk_rows(c, d, nch)
            ps = slice(hp * 2 * dk, (hp + 1) * 2 * dk)
            cum = cum_s[d, rows, ps]
            ref = cum[mid:mid + 1, :] if d == 0 else cum[CHUNK - 1 - mid:CHUNK - mid, :]
            tot = cum[CHUNK - 1:CHUNK, :] if d == 0 else cum[0:1, :]
            q = p[rows, q0 + hp * 2 * dk:q0 + (hp + 1) * 2 * dk] * qscale
            k = p[rows, k0 + hp * 2 * dk:k0 + (hp + 1) * 2 * dk]
            v = [p[rows, v0 + (2 * hp + h) * dv:v0 + (2 * hp + h + 1) * dv] for h in (0, 1)]
            incl = (col <= row) if d == 0 else (col >= row)
            kg_bd = _block_diag2(k * jnp.exp(ref - cum), lane_hi)
            scores = jnp.where(incl, _mm_nt(q * jnp.exp(cum - ref), kg_bd), 0.0)
            st.append(dict(rows=rows, scores=scores, v=v, qe=q * jnp.exp(cum), ke=k * jnp.exp(tot - cum),
                           e_tot=jnp.exp(tot)))

        for (d, hp), s in zip(pairs, st):
            y = yf if d == 0 else yb
            y[s["rows"], hp * 2 * dv:(hp + 1) * 2 * dv] = (_mm(s["scores"], _block_diag_wide(s["v"][0], s["v"][1]))
                                                          + _mm_nt(s["qe"], st_ref[d, hp]))
            for h in (0, 1):
                ke_h = jnp.where(lane_hi, s["ke"], 0.0) if h == 1 else jnp.where(lane_hi, 0.0, s["ke"])
                rs = slice(h * dv, (h + 1) * dv)
                st_ref[d, hp, rs, :] = st_ref[d, hp, rs, :] * s["e_tot"] + _mm_tn(s["v"][h], ke_h)
        return carry

    lax.fori_loop(0, nch, chunk, 0)


def _gla_scan(p, up, gb, nctx):
    scratch = [pltpu.VMEM((2, GLA_HEADS // 2, 2 * GLA_V, 2 * GLA_K), F32),
               pltpu.VMEM((2, TOKEN_BLOCK, GLA_HEADS * GLA_K), F32)]
    return _scan_call(_gla_kernel, p, (up, gb), scratch, nctx, False, "gla_scan")


def _merge_kernel(h_ref, mod_ref, nw_ref,
                  ssd_f, ssd_b, ml_f, ml_b, gd_f, gd_b, gl_f, gl_b,
                  z_ref, o_ref, gg_ref, lg_ref,
                  nssd, nml, ngd, ngl, wg_ref, wb_ref, wo_ref, out_ref):
    x = h_ref[...]
    xm = _norm_modulate(x, nw_ref[...], mod_ref[0:1, :], mod_ref[1:2, :]).astype(BF16)
    branches = (
        _group_rmsnorm((ssd_f[...] + ssd_b[...]) * _silu(z_ref[...]), nssd[...], SSD_GROUPS),
        _group_rmsnorm(ml_f[...] + ml_b[...], nml[...], MLSTM_HEADS) * _sigmoid(o_ref[...]),
        _group_rmsnorm(gd_f[...] + gd_b[...], ngd[...], GDN_HEADS) * _silu(gg_ref[...]),
        _group_rmsnorm(gl_f[...] + gl_b[...], ngl[...], GLA_HEADS) * _silu(lg_ref[...]),
    )
    u = None
    for n, br in enumerate(branches):
        gate = _sigmoid(jnp.dot(xm, wg_ref[:, n * D_MODEL:(n + 1) * D_MODEL], preferred_element_type=F32))
        term = gate * _mm(br, wb_ref[n])
        u = term if u is None else u + term
    out_ref[...] = x + mod_ref[2:3, :] * _mm(u, wo_ref[...])


def _merge(h, mods, norm_w, ys, p_ssd, p_ml, p_gd, p_gl, norms, wg, wb, wo, nctx):
    bsz, tt, d = h.shape
    grid, tok, mod = _token_grid_specs(bsz, tt, nctx)
    gate_spec = lambda blk: pl.BlockSpec((None, TOKEN_BLOCK, MIX_W), lambda b, t: (b, t, blk))
    return pl.pallas_call(
        _merge_kernel,
        grid=grid,
        in_specs=([tok(d), mod, _const_spec((1, d))] + [tok(MIX_W)] * 8
                  + [gate_spec(0), gate_spec(2), gate_spec(3), gate_spec(2)]
                  + [_const_spec((1, MIX_W))] * 4
                  + [_const_spec(wg.shape), _const_spec(wb.shape), _const_spec(wo.shape)]),
        out_specs=tok(d),
        out_shape=jax.ShapeDtypeStruct((bsz, tt, d), F32),
        compiler_params=pltpu.CompilerParams(dimension_semantics=("parallel", "parallel"),
                                             vmem_limit_bytes=VMEM_LIMIT),
        name="merge_out",
    )(h, mods, norm_w.reshape(1, d), *ys, p_ssd, p_ml, p_gd, p_gl,
      *[n.reshape(1, MIX_W) for n in norms], wg, wb, wo)


def _ffn_kernel(h_ref, mod_ref, nw_ref, wg_ref, wu_ref, wd_ref, out_ref):
    x = h_ref[...]
    xm = _norm_modulate(x, nw_ref[...], mod_ref[3:4, :], mod_ref[4:5, :]).astype(BF16)
    a = jnp.dot(xm, wg_ref[...], preferred_element_type=F32)
    b = jnp.dot(xm, wu_ref[...], preferred_element_type=F32)
    out_ref[...] = x + mod_ref[5:6, :] * _mm(_silu(a) * b, wd_ref[...])


def _ffn(h, mods, norm_w, wg, wu, wd, nctx):
    bsz, tt, d = h.shape
    grid, tok, mod = _token_grid_specs(bsz, tt, nctx)
    return pl.pallas_call(
        _ffn_kernel,
        grid=grid,
        in_specs=[tok(d), mod, _const_spec((1, d)), _const_spec(wg.shape), _const_spec(wu.shape),
                  _const_spec(wd.shape)],
        out_specs=tok(d),
        out_shape=jax.ShapeDtypeStruct((bsz, tt, d), F32),
        compiler_params=pltpu.CompilerParams(dimension_semantics=("parallel", "parallel"),
                                             vmem_limit_bytes=VMEM_LIMIT),
        name="swiglu",
    )(h, mods, norm_w.reshape(1, d), wg, wu, wd)


def _final_norm_kernel(h_ref, w_ref, o_ref):
    x = h_ref[...]
    o_ref[...] = x * lax.rsqrt(jnp.mean(x * x, axis=-1, keepdims=True) + 1e-6) * w_ref[...]


def _final_norm(h, w, ctx_len):
    bsz, tt, d = h.shape
    off = ctx_len // TOKEN_BLOCK
    nt = (tt - ctx_len) // TOKEN_BLOCK
    return pl.pallas_call(
        _final_norm_kernel,
        grid=(bsz, nt),
        in_specs=[pl.BlockSpec((None, TOKEN_BLOCK, d), lambda b, t: (b, t + off, 0)),
                  pl.BlockSpec((1, d), lambda b, t: (0, 0))],
        out_specs=pl.BlockSpec((None, TOKEN_BLOCK, d), lambda b, t: (b, t, 0)),
        out_shape=jax.ShapeDtypeStruct((bsz, tt - ctx_len, d), F32),
        name="final_norm",
    )(h, w.reshape(1, d))


def _pad_cols(w, total):
    return jnp.pad(w, ((0, 0), (0, total - w.shape[1])))


def _lane_row(values, lane0):
    v = values.reshape(-1).astype(F32)
    return jnp.zeros((1, LANES), F32).at[0, lane0:lane0 + v.shape[0]].set(v)


def _to_colmajor(x, ctx_len):
    bsz, tt, ch = x.shape
    lat = x[:, ctx_len:]
    rows = lat.shape[1] // GRID_W
    lat = lat.reshape(bsz, rows, GRID_W, ch).transpose(0, 2, 1, 3).reshape(bsz, rows * GRID_W, ch)
    return jnp.concatenate([x[:, :ctx_len], lat], axis=1)


def _from_colmajor(x, ctx_len):
    bsz, tt, ch = x.shape
    lat = x[:, ctx_len:]
    rows = lat.shape[1] // GRID_W
    lat = lat.reshape(bsz, GRID_W, rows, ch).transpose(0, 2, 1, 3).reshape(bsz, rows * GRID_W, ch)
    return jnp.concatenate([x[:, :ctx_len], lat], axis=1)


def kernel(x, c, ctx, c_ctx, ada_w, ada_b, norm_mix_w, w_in, ssd_conv_w, ssd_conv_b, ssd_dt_bias,
           ssd_a_log, ssd_d, ssd_norm_w, mlstm_i_bias, mlstm_f_bias, mlstm_norm_w, gdn_conv_w, gdn_a_log,
           gdn_dt_bias, gdn_norm_w, gla_gk_up, gla_gk_bias, gla_norm_w, w_branch, w_out, norm_ffn_w,
           ffn_w_gate, ffn_w_up, ffn_w_down, norm_f_w):
    bsz, seq, d = x.shape
    ctx_len = ctx.shape[1]
    depth = w_in.shape[0]
    assert d == D_MODEL and ctx_len % TOKEN_BLOCK == 0 and seq % TOKEN_BLOCK == 0 and seq % GRID_W == 0
    nctx = ctx_len // TOKEN_BLOCK

    cond = jnp.stack([jnp.broadcast_to(c_ctx, c.shape), c], axis=1).reshape(2 * bsz, d)
    mods_all = _mods(cond, ada_w, ada_b).reshape(depth, 2 * bsz, 6, d)

    ssd_cols = 2 * MIX_W + 2 * SSD_GROUPS * SSD_STATE + 2 * SSD_HEADS
    ml_cols = 2 * MLSTM_HEADS * MLSTM_K + 2 * MIX_W + 4 * MLSTM_HEADS
    gd_cols = 4 * MIX_W + 4 * GDN_HEADS
    gl_cols = 2 * GLA_HEADS * GLA_K + 2 * MIX_W + 2 * GLA_RANK
    o1, o2, o3, o4 = ssd_cols, ssd_cols + ml_cols, ssd_cols + ml_cols + gd_cols, ssd_cols + ml_cols + gd_cols + gl_cols

    h = jnp.concatenate([ctx, x], axis=1)
    for l in range(depth):
        wl = w_in[l]
        ws = (_pad_cols(wl[:, :o1], SSD_P).astype(BF16), _pad_cols(wl[:, o1:o2], MLSTM_P).astype(BF16),
              _pad_cols(wl[:, o2:o3], GDN_P).astype(BF16), _pad_cols(wl[:, o3:o4], GLA_P).astype(BF16))
        mods = mods_all[l]
        p_ssd, p_ml, p_gd, p_gl = _inproj(h, mods, norm_mix_w[l], ws, nctx)

        ssd_y = _ssd_scan(p_ssd, ssd_conv_w[l], ssd_conv_b[l].reshape(1, -1),
                          _lane_row(ssd_dt_bias[l], 0), _lane_row(-jnp.exp(ssd_a_log[l]), 0),
                          jnp.repeat(ssd_d[l], SSD_HEAD_DIM).reshape(1, MIX_W), nctx)
        ml_y = _mlstm_scan(_to_colmajor(p_ml, ctx_len), _lane_row(mlstm_i_bias[l], 0),
                           _lane_row(mlstm_f_bias[l], 2 * MLSTM_HEADS), nctx)
        gd_y = _gdn_scan(_to_colmajor(p_gd, ctx_len), gdn_conv_w[l],
                         _lane_row(gdn_dt_bias[l], 0), _lane_row(-jnp.exp(gdn_a_log[l]), 0), nctx)
        up = jnp.zeros((2, LANES, GLA_HEADS * GLA_K), F32)
        for dd in range(2):
            up = up.at[dd, dd * GLA_RANK:(dd + 1) * GLA_RANK].set(gla_gk_up[l, dd])
        gl_y = _gla_scan(p_gl, up.astype(BF16), gla_gk_bias[l], nctx)

        ys = (ssd_y[0], ssd_y[1], _from_colmajor(ml_y[0], ctx_len), _from_colmajor(ml_y[1], ctx_len),
              _from_colmajor(gd_y[0], ctx_len), _from_colmajor(gd_y[1], ctx_len), gl_y[0], gl_y[1])
        h = _merge(h, mods, norm_mix_w[l], ys, p_ssd, p_ml, p_gd, p_gl,
                   (ssd_norm_w[l], mlstm_norm_w[l], gdn_norm_w[l], gla_norm_w[l]),
                   wl[:, o4:].astype(BF16), w_branch[l].astype(BF16), w_out[l].astype(BF16), nctx)
        h = _ffn(h, mods, norm_ffn_w[l], ffn_w_gate[l].astype(BF16), ffn_w_up[l].astype(BF16),
                 ffn_w_down[l].astype(BF16), nctx)
    return _final_norm(h, norm_f_w, ctx_len)
```

```python
import functools
from typing import NamedTuple

import jax
import jax.numpy as jnp
from jax import lax
from jax.experimental import pallas as pl
from jax.experimental.pallas import tpu as pltpu

F32 = jnp.float32
BF16 = jnp.bfloat16
HIGHEST = lax.Precision.HIGHEST

D_MODEL = 1024
GRID_W = 64
CHUNK = 64
CHUNK_SHIFT = CHUNK.bit_length() - 1
CONV_K = 5
CONV_HALO = 8
MIX_W = 512
LANES = 128
TOKEN_BLOCK = 256
COLS_PER_BLOCK = 8
FFN_HIDDEN = 2816
VMEM_LIMIT = 56 * 1024 * 1024

SSD_HEADS, SSD_HEAD_DIM, SSD_STATE, SSD_GROUPS = 8, 64, 128, 2
MLSTM_HEADS, MLSTM_K, MLSTM_V = 4, 64, 128
GDN_HEADS, GDN_HEAD_DIM = 4, 128
GLA_HEADS, GLA_K, GLA_V, GLA_RANK, GLA_GATE_NORM = 4, 64, 128, 16, 16

SSD_P = 2 * MIX_W + 2 * SSD_GROUPS * SSD_STATE + LANES
MLSTM_P = 2 * MLSTM_HEADS * MLSTM_K + 2 * MIX_W + LANES
GDN_P = 4 * MIX_W + LANES
GLA_P = 2 * GLA_HEADS * GLA_K + 2 * MIX_W + LANES


class Geo(NamedTuple):
    nblk: int
    tb: int
    colmajor: bool
    c: int


def _mm(a, b):
    return jnp.dot(a.astype(BF16), b.astype(BF16), preferred_element_type=F32)


def _mm_nt(a, b):
    return lax.dot_general(a.astype(BF16), b.astype(BF16), (((1,), (1,)), ((), ())),
                           preferred_element_type=F32)


def _mm_tn(a, b):
    return lax.dot_general(a.astype(BF16), b.astype(BF16), (((0,), (0,)), ((), ())),
                           preferred_element_type=F32)


def _chunk_scan(x, d, op, identity):
    n = x.shape[0]
    pos = _iota2(x.shape, 0) & (CHUNK - 1)
    sh = 1
    while sh < CHUNK:
        if d == 0:
            x = op(x, jnp.where(pos >= sh, pltpu.roll(x, sh, axis=0), identity))
        else:
            x = op(x, jnp.where(pos < CHUNK - sh, pltpu.roll(x, n - sh, axis=0), identity))
        sh *= 2
    return x


def _chunk_cumsum(x, d):
    return _chunk_scan(x, d, jnp.add, 0.0)


def _rows_of(sel, x):
    return lax.dot_general(sel, x, (((1,), (1,)), ((), ())), precision=HIGHEST,
                           preferred_element_type=F32)


def _sigmoid(x):
    return 1.0 / (1.0 + jnp.exp(-x))


def _silu(x):
    return x * _sigmoid(x)


def _softplus(x):
    return jnp.maximum(x, 0.0) + jnp.log1p(jnp.exp(-jnp.abs(x)))


def _log_sigmoid(x):
    return jnp.minimum(x, 0.0) - jnp.log1p(jnp.exp(-jnp.abs(x)))


def _iota2(shape, axis):
    return lax.broadcasted_iota(jnp.int32, shape, axis)


def _one_hot_rows(nrows, lane0):
    r, c = _iota2((nrows, LANES), 0), _iota2((nrows, LANES), 1)
    return jnp.where(c == r + lane0, 1.0, 0.0).astype(F32)


def _chunk_rows(c, d, nch):
    r0 = c * CHUNK if d == 0 else (nch - 1 - c) * CHUNK
    return pl.ds(pl.multiple_of(r0, CHUNK), CHUNK)


def _pair_cols(x, lane_hi, la):
    return jnp.where(lane_hi, x[:, la + 1:la + 2], x[:, la:la + 1])


def _block_diag2(x, lane_hi):
    return jnp.concatenate([jnp.where(lane_hi, 0.0, x), jnp.where(lane_hi, x, 0.0)], axis=0)


def _block_diag_wide(a, b):
    z = jnp.zeros_like(a)
    return jnp.concatenate([jnp.concatenate([a, z], axis=1), jnp.concatenate([z, b], axis=1)], axis=0)


def _group_rmsnorm(x, w, groups, eps=1e-6):
    gw = x.shape[-1] // groups
    outs = []
    for g in range(groups):
        xg = x[:, g * gw:(g + 1) * gw]
        ms = jnp.mean(xg * xg, axis=-1, keepdims=True)
        outs.append(xg * lax.rsqrt(ms + eps))
    return jnp.concatenate(outs, axis=-1) * w


def _norm_modulate(x, nw, shift, scale, eps=1e-6):
    ms = jnp.mean(x * x, axis=-1, keepdims=True)
    y = x * lax.rsqrt(ms + eps) * nw
    return y * (1.0 + scale) + shift


def _stage(dst_ref, row0, p_ref, c0, c1, geo):
    if not geo.colmajor:
        dst_ref[row0:row0 + geo.tb, :] = p_ref[:, c0:c1]
    else:
        for j in range(geo.tb // CHUNK):
            dst_ref[row0 + j * CHUNK:row0 + (j + 1) * CHUNK, :] = p_ref[:, j * geo.c + c0:j * geo.c + c1]


def _unstage(y_ref, ysc_ref, geo):
    w = ysc_ref.shape[-1]
    for j in range(geo.tb // CHUNK):
        y_ref[:, j * w:(j + 1) * w] = ysc_ref[j * CHUNK:(j + 1) * CHUNK, :]


def _conv_silu(dst_ref, xpad_ref, p_ref, prev_ref, next_ref, col0, width, w_ref, b_ref,
               left_valid, right_valid, geo):
    tb = geo.tb
    base = CONV_HALO - CONV_K // 2
    win = CHUNK + 2 * CONV_HALO
    for s in range(width // LANES):
        c0 = col0 + s * LANES
        ocols = slice(s * LANES, (s + 1) * LANES)
        xpad_ref[0:CONV_HALO, :] = jnp.where(left_valid, prev_ref[:, c0:c0 + LANES], 0.0)
        _stage(xpad_ref, CONV_HALO, p_ref, c0, c0 + LANES, geo)
        xpad_ref[CONV_HALO + tb:2 * CONV_HALO + tb, :] = jnp.where(right_valid, next_ref[:, c0:c0 + LANES], 0.0)
        taps = [w_ref[k:k + 1, ocols] for k in range(CONV_K)]
        bias = None if b_ref is None else b_ref[:, ocols]

        def tile(rt, carry):
            r0 = pl.multiple_of(rt * CHUNK, CHUNK)
            acc = taps[0] * xpad_ref[pl.ds(r0 + base, CHUNK), :]
            for k in range(1, CONV_K):
                acc = acc + taps[k] * xpad_ref[pl.ds(r0 + base + k, CHUNK), :]
            if bias is not None:
                acc = acc + bias
            dst_ref[pl.ds(r0, CHUNK), ocols] = _silu(acc)
            return carry

        lax.fori_loop(0, tb // CHUNK, tile, 0, unroll=2)


def _mods_kernel(c_ref, w_ref, b_ref, o_ref):
    o_ref[...] = _mm(_silu(c_ref[...]), w_ref[...]) + b_ref[...]


def _mods(cond, ada_w, ada_b):
    depth, d, n = ada_w.shape
    r = cond.shape[0]
    nt = n // d
    return pl.pallas_call(
        _mods_kernel,
        grid=(depth, nt),
        in_specs=[pl.BlockSpec((r, d), lambda l, j: (0, 0)),
                  pl.BlockSpec((None, d, d), lambda l, j: (l, 0, j)),
                  pl.BlockSpec((None, 1, d), lambda l, j: (l, 0, j))],
        out_specs=pl.BlockSpec((None, r, d), lambda l, j: (l, 0, j)),
        out_shape=jax.ShapeDtypeStruct((depth, r, n), F32),
        name="adaln_mods",
    )(cond, ada_w, ada_b.reshape(depth, 1, n))


def _token_grid_specs(bsz, t, stream):
    tok = lambda width: pl.BlockSpec((None, TOKEN_BLOCK, width), lambda b, i: (b, i, 0))
    mod = pl.BlockSpec((None, 6, D_MODEL), lambda b, i: (2 * b + stream, 0, 0))
    return (bsz, t // TOKEN_BLOCK), tok, mod


def _const_spec(shape):
    zeros = (0,) * len(shape)
    return pl.BlockSpec(shape, lambda b, i: zeros, pipeline_mode=pl.Buffered(1))


_TOKEN_PARAMS = pltpu.CompilerParams(dimension_semantics=("parallel", "parallel"), vmem_limit_bytes=VMEM_LIMIT)


def _inproj_kernel(h_ref, mod_ref, nw_ref, w0, w1, w2, w3, o0, o1, o2, o3):
    xm = _norm_modulate(h_ref[...], nw_ref[...], mod_ref[0:1, :], mod_ref[1:2, :]).astype(BF16)
    for w, o in ((w0, o0), (w1, o1), (w2, o2), (w3, o3)):
        o[...] = jnp.dot(xm, w[...], preferred_element_type=F32)


def _inproj(h, mods, norm_w, ws, stream):
    bsz, t, d = h.shape
    grid, tok, mod = _token_grid_specs(bsz, t, stream)
    return pl.pallas_call(
        _inproj_kernel,
        grid=grid,
        in_specs=[tok(d), mod, _const_spec((1, d))] + [_const_spec(w.shape) for w in ws],
        out_specs=[tok(w.shape[1]) for w in ws],
        out_shape=[jax.ShapeDtypeStruct((bsz, t, w.shape[1]), F32) for w in ws],
        compiler_params=_TOKEN_PARAMS,
        name="in_proj",
    )(h, mods, norm_w.reshape(1, d), *ws)


def _scan_call(kernel, p, params, states, scratch_fn, colmajor, conv, name):
    bsz, t, c = p.shape
    if colmajor:
        rows = t // GRID_W
        assert rows == CHUNK and GRID_W % COLS_PER_BLOCK == 0
        geo = Geo(GRID_W // COLS_PER_BLOCK, COLS_PER_BLOCK * CHUNK, True, c)
        pv = p.reshape(bsz, rows, GRID_W * c)
        yshape = (bsz, rows, GRID_W * MIX_W)
        last_halo = rows // CONV_HALO - 1

        def specs(blk):
            main = pl.BlockSpec((None, rows, COLS_PER_BLOCK * c), lambda b, i: (b, 0, blk(i)))
            prev = pl.BlockSpec((None, CONV_HALO, c),
                                lambda b, i: (b, last_halo, jnp.maximum(blk(i) * COLS_PER_BLOCK - 1, 0)))
            nxt = pl.BlockSpec((None, CONV_HALO, c),
                               lambda b, i: (b, 0, jnp.minimum((blk(i) + 1) * COLS_PER_BLOCK, GRID_W - 1)))
            return [main, prev, nxt] if conv else [main]

        yspec = lambda blk: pl.BlockSpec((None, rows, COLS_PER_BLOCK * MIX_W), lambda b, i: (b, 0, blk(i)))
    else:
        assert t % TOKEN_BLOCK == 0
        geo = Geo(t // TOKEN_BLOCK, TOKEN_BLOCK, False, c)
        pv = p
        yshape = (bsz, t, MIX_W)
        per_halo = TOKEN_BLOCK // CONV_HALO
        nhalo = t // CONV_HALO

        def specs(blk):
            main = pl.BlockSpec((None, TOKEN_BLOCK, c), lambda b, i: (b, blk(i), 0))
            prev = pl.BlockSpec((None, CONV_HALO, c), lambda b, i: (b, jnp.maximum(blk(i) * per_halo - 1, 0), 0))
            nxt = pl.BlockSpec((None, CONV_HALO, c),
                               lambda b, i: (b, jnp.minimum((blk(i) + 1) * per_halo, nhalo - 1), 0))
            return [main, prev, nxt] if conv else [main]

        yspec = lambda blk: pl.BlockSpec((None, TOKEN_BLOCK, MIX_W), lambda b, i: (b, blk(i), 0))

    fwd = lambda i: i
    rev = lambda i: geo.nblk - 1 - i
    n_p = 3 if conv else 1
    state_specs = [pl.BlockSpec((None,) + s.shape[1:], lambda b, i, n=s.ndim - 1: (b,) + (0,) * n) for s in states]
    yout = jax.ShapeDtypeStruct(yshape, F32)
    outs = pl.pallas_call(
        functools.partial(kernel, geo),
        grid=(bsz, geo.nblk),
        in_specs=specs(fwd) + specs(rev) + [_const_spec(w.shape) for w in params] + state_specs,
        out_specs=[yspec(fwd), yspec(rev)] + state_specs,
        out_shape=[yout, yout] + [jax.ShapeDtypeStruct(s.shape, F32) for s in states],
        scratch_shapes=scratch_fn(geo),
        compiler_params=pltpu.CompilerParams(dimension_semantics=("parallel", "arbitrary"),
                                             vmem_limit_bytes=VMEM_LIMIT),
        name=name,
    )(*([pv] * (2 * n_p)), *params, *states)
    return outs[0].reshape(bsz, t, MIX_W), outs[1].reshape(bsz, t, MIX_W), tuple(outs[2:])


def _y_targets(geo, yf, yb, ysc):
    return (ysc.at[0], ysc.at[1]) if geo.colmajor else (yf, yb)


def _y_finish(geo, yf, yb, ysc):
    if geo.colmajor:
        _unstage(yf, ysc.at[0], geo)
        _unstage(yb, ysc.at[1], geo)


def _ssd_kernel(geo, pf, pfp, pfn, pb, pbp, pbn, cw, cb, dtb, nega, dskip, s_in,
                yf, yb, s_ref, xpad, xbc, cum_s, v_s):
    i = pl.program_id(1)
    tb = geo.tb
    nch = tb // CHUNK
    hd = SSD_HEAD_DIM
    assert hd == CHUNK

    @pl.when(i == 0)
    def _():
        s_ref[...] = s_in[...]

    for d, (p, pp, pn) in enumerate(((pf, pfp, pfn), (pb, pbp, pbn))):
        j = i if d == 0 else geo.nblk - 1 - i
        _conv_silu(xbc.at[d], xpad, p, pp, pn, MIX_W, 2 * MIX_W, cw, cb, j != 0, j != geo.nblk - 1, geo)
        delta = _softplus(p[:, 3 * MIX_W:3 * MIX_W + LANES] + dtb[...])
        cum_s[d] = _chunk_cumsum(nega[...] * delta, d)
        for h in range(SSD_HEADS):
            lane = d * SSD_HEADS + h
            v_s[d, :, h * hd:(h + 1) * hd] = xbc[d, :, h * hd:(h + 1) * hd] * delta[:, lane:lane + 1]

    hpg = SSD_HEADS // SSD_GROUPS
    gw = hpg * hd
    groups = [(d, g) for d in (0, 1) for g in range(SSD_GROUPS)]
    blk = _iota2((CHUNK, gw), 1) >> CHUNK_SHIFT
    col = _iota2((CHUNK, gw), 1) & (CHUNK - 1)
    row = _iota2((CHUNK, gw), 0)

    def group_cols(x, base):
        out = x[:, base:base + 1]
        for j in range(1, hpg):
            out = jnp.where(blk[0:x.shape[0]] == j, x[:, base + j:base + j + 1], out)
        return out

    def chunk(c, carry):
        dirs = []
        for d in (0, 1):
            rows = _chunk_rows(c, d, nch)
            cum = cum_s[d, rows, :]
            tot = cum[CHUNK - 1:CHUNK, :] if d == 0 else cum[0:1, :]
            cum_t = _rows_of(_one_hot_rows(SSD_HEADS, d * SSD_HEADS), jnp.concatenate([cum] * hpg, axis=0))
            dirs.append(dict(rows=rows, cum=cum, tot=tot, cum_t=cum_t))

        st = []
        for d, g in groups:
            dd = dirs[d]
            rows = dd["rows"]
            base = d * SSD_HEADS + g * hpg
            incl = (col <= row) if d == 0 else (col >= row)
            ccol = group_cols(dd["cum"], base)
            crow = dd["cum_t"][g * hpg:g * hpg + 1, :]
            for j in range(1, hpg):
                crow = jnp.where(blk[0:1] == j, dd["cum_t"][g * hpg + j:g * hpg + j + 1, :], crow)
            tot = group_cols(dd["tot"], base)
            decay = jnp.exp(jnp.where(incl, ccol - crow, -jnp.inf))
            bg = xbc[d, rows, MIX_W + g * SSD_STATE:MIX_W + (g + 1) * SSD_STATE]
            cg = xbc[d, rows, MIX_W + (SSD_GROUPS + g) * SSD_STATE:MIX_W + (SSD_GROUPS + g + 1) * SSD_STATE]
            vg = v_s[d, rows, g * gw:(g + 1) * gw]
            scores = _mm_nt(cg, jnp.concatenate([bg] * hpg, axis=0))
            inter = _mm(cg, s_ref[d, g])
            v_bd = jnp.concatenate([jnp.where(blk == j, vg, 0.0) for j in range(hpg)], axis=0)
            st.append(dict(p=scores * decay, inter=inter * jnp.exp(ccol), v_bd=v_bd, bg=bg,
                           v_end=vg * jnp.exp(tot - ccol), e_tot=jnp.exp(tot)))

        for (d, g), s in zip(groups, st):
            rows = dirs[d]["rows"]
            y = yf if d == 0 else yb
            yg = _mm(s["p"], s["v_bd"]) + s["inter"]
            if d == 0:
                yg = yg + xbc[d, rows, g * gw:(g + 1) * gw] * dskip[:, g * gw:(g + 1) * gw]
            y[rows, g * gw:(g + 1) * gw] = yg
            s_ref[d, g] = s_ref[d, g] * s["e_tot"] + _mm_tn(s["bg"], s["v_end"])
        return carry

    lax.fori_loop(0, nch, chunk, 0)


def _ssd_scan(p, cw, cb, dtb, nega, dskip, state):
    def scratch(geo):
        return [pltpu.VMEM((geo.tb + 2 * CONV_HALO, LANES), F32),
                pltpu.VMEM((2, geo.tb, 2 * MIX_W), F32),
                pltpu.VMEM((2, geo.tb, LANES), F32),
                pltpu.VMEM((2, geo.tb, MIX_W), F32)]
    return _scan_call(_ssd_kernel, p, (cw, cb, dtb, nega, dskip), state, scratch, False, True, "ssd_scan")


def _ssd_state(bsz):
    return (jnp.zeros((bsz, 2, SSD_GROUPS, SSD_STATE, MIX_W // SSD_GROUPS), F32),)


def _mlstm_kernel(geo, pf, pb, ib, fb, c_in, m_in, yf, yb, c_ref, m_ref, qkv, b_s, li_s, cm_s, ysc):
    i = pl.program_id(1)
    tb = geo.tb
    nch = tb // CHUNK
    nh, dk, dv = MLSTM_HEADS, MLSTM_K, MLSTM_V
    kscale = dk ** -0.5
    k0, v0, g0 = nh * dk, 2 * nh * dk, 2 * nh * dk + 2 * MIX_W
    fl = 2 * nh

    @pl.when(i == 0)
    def _():
        c_ref[...] = c_in[...]
        m_ref[...] = m_in[...]

    for d, p in enumerate((pf, pb)):
        _stage(qkv.at[d], 0, p, 0, g0 - MIX_W, geo)
        _stage(li_s.at[d], 0, p, g0, g0 + LANES, geo)
        gates = li_s[d]
        li = pltpu.roll(gates + ib[...], fl, axis=1)
        b = _chunk_cumsum(_log_sigmoid(gates + fb[...]), d)
        li_s[d] = li
        b_s[d] = b
        cm_s[d] = _chunk_scan(li - b, d, jnp.maximum, -jnp.inf)

    ones_col = jnp.where(_iota2((CHUNK, dv), 1) == 0, 1.0, 0.0).astype(F32)
    npair = nh // 2
    pairs = [(d, hp) for d in (0, 1) for hp in range(npair)]
    lane_hi = _iota2((CHUNK, 2 * CHUNK), 1) >= CHUNK
    col = _iota2((CHUNK, 2 * CHUNK), 1) & (CHUNK - 1)
    row = _iota2((CHUNK, 2 * CHUNK), 0)
    ew = 2 * dv
    ydst = _y_targets(geo, yf, yb, ysc)

    def chunk(c, carry):
        dirs = []
        for d in (0, 1):
            rows = _chunk_rows(c, d, nch)
            sel = _one_hot_rows(8, fl + d * nh)
            b = b_s[d, rows, :]
            li = li_s[d, rows, :]
            tot = b[CHUNK - 1:CHUNK, :] if d == 0 else b[0:1, :]
            m_row = m_ref[d:d + 1, :]
            end_log = tot - b + li
            m_new = jnp.maximum(tot + m_row, jnp.max(end_log, axis=0, keepdims=True))
            m_ref[d:d + 1, :] = m_new
            inter = b + m_row
            bli = b - li
            dirs.append(dict(rows=rows, b=b, inter=inter, m_t=jnp.maximum(inter, b + cm_s[d, rows, :]),
                             bli_t=_rows_of(sel, jnp.concatenate([bli, bli], axis=0)),
                             w_end=jnp.exp(end_log - m_new), keep=jnp.exp(tot + m_row - m_new)))

        st = []
        for d, hp in pairs:
            g = dirs[d]
            la = fl + d * nh + 2 * hp
            incl = (col <= row) if d == 0 else (col >= row)
            bli_row = jnp.where(lane_hi[0:1], g["bli_t"][2 * hp + 1:2 * hp + 2, :], g["bli_t"][2 * hp:2 * hp + 1, :])
            d_log = jnp.where(incl, _pair_cols(g["b"], lane_hi, la) - bli_row, -jnp.inf)
            m_pair = _pair_cols(g["m_t"], lane_hi, la)
            q = qkv[d, g["rows"], hp * 2 * dk:(hp + 1) * 2 * dk]
            k = qkv[d, g["rows"], k0 + hp * 2 * dk:k0 + (hp + 1) * 2 * dk] * kscale
            vext = [jnp.concatenate([qkv[d, g["rows"], v0 + (2 * hp + h) * dv:v0 + (2 * hp + h + 1) * dv], ones_col],
                                    axis=1) for h in (0, 1)]
            scores = _mm_nt(q, _block_diag2(k, lane_hi)) * jnp.exp(d_log - m_pair)
            qw = q * jnp.exp(_pair_cols(g["inter"], lane_hi, la) - m_pair)
            st.append(dict(lhs=jnp.concatenate([scores, qw], axis=1), k=k, vext=vext))

        res = [_mm(s["lhs"], jnp.concatenate([_block_diag_wide(s["vext"][0], s["vext"][1]), c_ref[d, hp]], axis=0))
               for (d, hp), s in zip(pairs, st)]

        for (d, hp), s, r in zip(pairs, st, res):
            g = dirs[d]
            la = fl + d * nh + 2 * hp
            kw = s["k"] * _pair_cols(g["w_end"], lane_hi, la)
            for h in (0, 1):
                den = jnp.maximum(jnp.abs(r[:, h * ew + dv:h * ew + dv + 1]), jnp.exp(-g["m_t"][:, la + h:la + h + 1]))
                ydst[d][g["rows"], (2 * hp + h) * dv:(2 * hp + h + 1) * dv] = r[:, h * ew:h * ew + dv] / den
                kw_h = jnp.where(lane_hi, kw, 0.0) if h == 1 else jnp.where(lane_hi, 0.0, kw)
                cs = slice(h * ew, (h + 1) * ew)
                c_ref[d, hp, :, cs] = c_ref[d, hp, :, cs] * g["keep"][:, la + h:la + h + 1] + _mm_tn(kw_h, s["vext"][h])
        return carry

    lax.fori_loop(0, nch, chunk, 0)
    _y_finish(geo, yf, yb, ysc)


def _mlstm_scan(p, ib, fb, state, colmajor):
    def scratch(geo):
        return [pltpu.VMEM((2, geo.tb, 2 * MLSTM_HEADS * MLSTM_K + MIX_W), F32),
                pltpu.VMEM((2, geo.tb, LANES), F32),
                pltpu.VMEM((2, geo.tb, LANES), F32),
                pltpu.VMEM((2, geo.tb, LANES), F32),
                pltpu.VMEM((2, geo.tb, MIX_W) if geo.colmajor else (2, 8, LANES), F32)]
    return _scan_call(_mlstm_kernel, p, (ib, fb), state, scratch, colmajor, False, "mlstm_scan")


def _mlstm_state(bsz):
    return (jnp.zeros((bsz, 2, MLSTM_HEADS // 2, 2 * MLSTM_K, 4 * MLSTM_V), F32), jnp.zeros((bsz, 8, LANES), F32))


GDN_PREP_CHUNKS = 4


def _gdn_kernel(geo, pf, pfp, pfn, pb, pbp, pbn, cw, dtb, nega, s_in, yf, yb,
                s_ref, xpad, qkv, cum_s, beta_s, lvl, sol_s, attn_s, ysc):
    i = pl.program_id(1)
    tb = geo.tb
    nch = tb // CHUNK
    nh, hd = GDN_HEADS, GDN_HEAD_DIM
    g0 = 4 * MIX_W
    npair = nh // 2
    pw = 2 * hd

    @pl.when(i == 0)
    def _():
        s_ref[...] = s_in[...]
        r, c = _iota2((CHUNK, 2 * CHUNK), 0), _iota2((CHUNK, 2 * CHUNK), 1) & (CHUNK - 1)
        for k in range(lvl.shape[0]):
            joined = ((r >> (k + 1)) == (c >> (k + 1))) & (((r >> k) & 1) != ((c >> k) & 1))
            lvl[k] = jnp.where(joined, 1.0, 0.0).astype(F32)

    for d, (p, pp, pn) in enumerate(((pf, pfp, pfn), (pb, pbp, pbn))):
        j = i if d == 0 else geo.nblk - 1 - i
        _conv_silu(qkv.at[d], xpad, p, pp, pn, 0, 3 * MIX_W, cw, None, j != 0, j != geo.nblk - 1, geo)
        for h in range(2 * nh):
            hs = slice(h * hd, (h + 1) * hd)
            x = qkv[d, :, hs]
            x = x * lax.rsqrt(jnp.sum(x * x, axis=-1, keepdims=True) + 1e-6)
            qkv[d, :, hs] = x * (hd ** -0.5) if h < nh else x
        _stage(beta_s.at[d], 0, p, g0, g0 + LANES, geo)
        gates = beta_s[d]
        cum_s[d] = _chunk_cumsum(nega[...] * _softplus(gates + dtb[...]), d)
        beta_s[d] = pltpu.roll(_sigmoid(gates), LANES - 2 * nh, axis=1)

    lane_hi = _iota2((CHUNK, 2 * CHUNK), 1) >= CHUNK
    col = _iota2((CHUNK, 2 * CHUNK), 1) & (CHUNK - 1)
    row = _iota2((CHUNK, 2 * CHUNK), 0)
    ydst = _y_targets(geo, yf, yb, ysc)

    def gate_cols(c, d):
        rows = _chunk_rows(c, d, nch)
        cum = cum_s[d, rows, :]
        tot = cum[CHUNK - 1:CHUNK, :] if d == 0 else cum[0:1, :]
        return rows, cum, tot

    def heads(d, rows, base, hp):
        return [qkv[d, rows, base + (2 * hp + h) * hd:base + (2 * hp + h + 1) * hd] for h in (0, 1)]

    def prep(it, carry):
        dirs = {}
        for cc in range(GDN_PREP_CHUNKS):
            for d in (0, 1):
                rows, cum, tot = gate_cols(it * GDN_PREP_CHUNKS + cc, d)
                cum_t = _rows_of(_one_hot_rows(8, d * nh), jnp.concatenate([cum, cum], axis=0))
                dirs[cc, d] = dict(rows=rows, cum=cum, beta=beta_s[d, rows, :], cum_t=cum_t, e_cum=jnp.exp(cum))
        units = [(cc, d, hp) for cc in range(GDN_PREP_CHUNKS) for d in (0, 1) for hp in range(npair)]

        st = []
        for cc, d, hp in units:
            g = dirs[cc, d]
            la = d * nh + 2 * hp
            incl = (col <= row) if d == 0 else (col >= row)
            strict = (col < row) if d == 0 else (col > row)
            crow = jnp.where(lane_hi[0:1], g["cum_t"][2 * hp + 1:2 * hp + 2, :], g["cum_t"][2 * hp:2 * hp + 1, :])
            decay = jnp.exp(jnp.where(incl, _pair_cols(g["cum"], lane_hi, la) - crow, -jnp.inf))
            q, k, v = (heads(d, g["rows"], base, hp) for base in (0, MIX_W, 2 * MIX_W))
            bcol = [g["beta"][:, la + h:la + h + 1] for h in (0, 1)]
            ecol = [g["e_cum"][:, la + h:la + h + 1] for h in (0, 1)]
            kb = [k[h] * bcol[h] for h in (0, 1)]
            lhs = jnp.concatenate([jnp.concatenate(kb, axis=1), jnp.concatenate(q, axis=1)], axis=0)
            prod = _mm_nt(lhs, _block_diag_wide(k[0], k[1]))
            attn_s[d, hp, g["rows"], :] = prod[CHUNK:2 * CHUNK] * decay
            rhs = [jnp.concatenate([kb[h] * ecol[h], v[h] * bcol[h]], axis=1) for h in (0, 1)]
            st.append(dict(a=jnp.where(strict, prod[0:CHUNK] * decay, 0.0), rhs=rhs))

        es = [-(s["a"] * lvl[0]) for s in st]
        for lv in range(1, lvl.shape[0]):
            aks = [s["a"] * lvl[lv] for s in st]
            xs = [ak + _mm(ak, _block_diag2(e, lane_hi)) for ak, e in zip(aks, es)]
            es = [e - (x + _mm(e, _block_diag2(x, lane_hi))) for e, x in zip(es, xs)]

        for (cc, d, hp), s, e in zip(units, st, es):
            rhs = s["rhs"]
            sol_s[d, hp, dirs[cc, d]["rows"], :] = (jnp.concatenate(rhs, axis=1)
                                                   + _mm(e, _block_diag_wide(rhs[0], rhs[1])))
        return carry

    lax.fori_loop(0, nch // GDN_PREP_CHUNKS, prep, 0)

    pairs = [(d, hp) for d in (0, 1) for hp in range(npair)]

    def chunk(c, carry):
        dirs = []
        for d in (0, 1):
            rows, cum, tot = gate_cols(c, d)
            dirs.append(dict(rows=rows, e_cum=jnp.exp(cum), e_end=jnp.exp(tot - cum), e_tot=jnp.exp(tot)))
        rs, sols, ks = [], [], []
        for d, hp in pairs:
            g = dirs[d]
            la = d * nh + 2 * hp
            sol = sol_s[d, hp, g["rows"], :]
            q = heads(d, g["rows"], 0, hp)
            qe = jnp.concatenate([q[h] * g["e_cum"][:, la + h:la + h + 1] for h in (0, 1)], axis=1)
            w = jnp.concatenate([sol[:, 0:hd], sol[:, 2 * hd:3 * hd]], axis=1)
            rs.append(_mm(jnp.concatenate([w, qe], axis=0), s_ref[d, hp]))
            sols.append(sol)
        for (d, hp), sol, r in zip(pairs, sols, rs):
            g = dirs[d]
            la = d * nh + 2 * hp
            k = heads(d, g["rows"], MIX_W, hp)
            v_new = [sol[:, hd:2 * hd] - r[0:CHUNK, 0:hd], sol[:, 3 * hd:4 * hd] - r[0:CHUNK, hd:pw]]
            ydst[d][g["rows"], hp * pw:(hp + 1) * pw] = (
                r[CHUNK:2 * CHUNK] + _mm(attn_s[d, hp, g["rows"], :], _block_diag_wide(v_new[0], v_new[1])))
            for h in (0, 1):
                blk = slice(h * hd, (h + 1) * hd)
                k_end = k[h] * g["e_end"][:, la + h:la + h + 1]
                s_ref[d, hp, blk, blk] = (s_ref[d, hp, blk, blk] * g["e_tot"][:, la + h:la + h + 1]
                                          + _mm_tn(k_end, v_new[h]))
        return carry

    lax.fori_loop(0, nch, chunk, 0)
    _y_finish(geo, yf, yb, ysc)


def _gdn_scan(p, cw, dtb, nega, state, colmajor):
    nlevels = CHUNK.bit_length() - 1
    pw = 2 * GDN_HEAD_DIM

    def scratch(geo):
        return [pltpu.VMEM((geo.tb + 2 * CONV_HALO, LANES), F32),
                pltpu.VMEM((2, geo.tb, 3 * MIX_W), F32),
                pltpu.VMEM((2, geo.tb, LANES), F32),
                pltpu.VMEM((2, geo.tb, LANES), F32),
                pltpu.VMEM((nlevels, CHUNK, 2 * CHUNK), F32),
                pltpu.VMEM((2, GDN_HEADS // 2, geo.tb, 2 * pw), F32),
                pltpu.VMEM((2, GDN_HEADS // 2, geo.tb, 2 * CHUNK), F32),
                pltpu.VMEM((2, geo.tb, MIX_W) if geo.colmajor else (2, 8, LANES), F32)]
    return _scan_call(_gdn_kernel, p, (cw, dtb, nega), state, scratch, colmajor, True, "gdn_scan")


def _gdn_state(bsz):
    return (jnp.zeros((bsz, 2, GDN_HEADS // 2, 2 * GDN_HEAD_DIM, 2 * GDN_HEAD_DIM), F32),)


def _gla_kernel(geo, pf, pb, up, gb, st_in, yf, yb, st_ref, cum_s):
    i = pl.program_id(1)
    tb = geo.tb
    nch = tb // CHUNK
    nh, dk, dv = GLA_HEADS, GLA_K, GLA_V
    qscale = dk ** -0.5
    q0, k0, v0, g0 = 0, nh * dk, 2 * nh * dk, 2 * nh * dk + 2 * MIX_W
    mid = CHUNK // 2

    @pl.when(i == 0)
    def _():
        st_ref[...] = st_in[...]

    for d, p in enumerate((pf, pb)):
        gk = _mm(p[:, g0:g0 + LANES], up[d]) + gb[d:d + 1, :]
        cum_s[d] = _chunk_cumsum(_log_sigmoid(gk) * (1.0 / GLA_GATE_NORM), d)

    npair = nh // 2
    pairs = [(d, hp) for d in (0, 1) for hp in range(npair)]
    lane_hi = _iota2((CHUNK, 2 * CHUNK), 1) >= CHUNK
    col = _iota2((CHUNK, 2 * CHUNK), 1) & (CHUNK - 1)
    row = _iota2((CHUNK, 2 * CHUNK), 0)
    assert dk == CHUNK

    def chunk(c, carry):
        st = []
        for d, hp in pairs:
            p = pf if d == 0 else pb
            rows = _chunk_rows(c, d, nch)
            ps = slice(hp * 2 * dk, (hp + 1) * 2 * dk)
            cum = cum_s[d, rows, ps]
            ref = cum[mid:mid + 1, :] if d == 0 else cum[CHUNK - 1 - mid:CHUNK - mid, :]
            tot = cum[CHUNK - 1:CHUNK, :] if d == 0 else cum[0:1, :]
            q = p[rows, q0 + hp * 2 * dk:q0 + (hp + 1) * 2 * dk] * qscale
            k = p[rows, k0 + hp * 2 * dk:k0 + (hp + 1) * 2 * dk]
            v = [p[rows, v0 + (2 * hp + h) * dv:v0 + (2 * hp + h + 1) * dv] for h in (0, 1)]
            incl = (col <= row) if d == 0 else (col >= row)
            kg_bd = _block_diag2(k * jnp.exp(ref - cum), lane_hi)
            scores = jnp.where(incl, _mm_nt(q * jnp.exp(cum - ref), kg_bd), 0.0)
            st.append(dict(rows=rows, scores=scores, v=v, qe=q * jnp.exp(cum), ke=k * jnp.exp(tot - cum),
                           e_tot=jnp.exp(tot)))

        for (d, hp), s in zip(pairs, st):
            y = yf if d == 0 else yb
            y[s["rows"], hp * 2 * dv:(hp + 1) * 2 * dv] = (_mm(s["scores"], _block_diag_wide(s["v"][0], s["v"][1]))
                                                          + _mm_nt(s["qe"], st_ref[d, hp]))
            for h in (0, 1):
                ke_h = jnp.where(lane_hi, s["ke"], 0.0) if h == 1 else jnp.where(lane_hi, 0.0, s["ke"])
                rs = slice(h * dv, (h + 1) * dv)
                st_ref[d, hp, rs, :] = st_ref[d, hp, rs, :] * s["e_tot"] + _mm_tn(s["v"][h], ke_h)
        return carry

    lax.fori_loop(0, nch, chunk, 0)


def _gla_scan(p, up, gb, state):
    def scratch(geo):
        return [pltpu.VMEM((2, geo.tb, GLA_HEADS * GLA_K), F32)]
    return _scan_call(_gla_kernel, p, (up, gb), state, scratch, False, False, "gla_scan")


def _gla_state(bsz):
    return (jnp.zeros((bsz, 2, GLA_HEADS // 2, 2 * GLA_V, 2 * GLA_K), F32),)


def _merge_kernel(h_ref, mod_ref, nw_ref,
                  ssd_f, ssd_b, ml_f, ml_b, gd_f, gd_b, gl_f, gl_b,
                  z_ref, o_ref, gg_ref, lg_ref,
                  nssd, nml, ngd, ngl, wg_ref, wb_ref, wo_ref, out_ref):
    x = h_ref[...]
    xm = _norm_modulate(x, nw_ref[...], mod_ref[0:1, :], mod_ref[1:2, :]).astype(BF16)
    branches = (
        _group_rmsnorm((ssd_f[...] + ssd_b[...]) * _silu(z_ref[...]), nssd[...], SSD_GROUPS),
        _group_rmsnorm(ml_f[...] + ml_b[...], nml[...], MLSTM_HEADS) * _sigmoid(o_ref[...]),
        _group_rmsnorm(gd_f[...] + gd_b[...], ngd[...], GDN_HEADS) * _silu(gg_ref[...]),
        _group_rmsnorm(gl_f[...] + gl_b[...], ngl[...], GLA_HEADS) * _silu(lg_ref[...]),
    )
    u = None
    for n, br in enumerate(branches):
        gate = _sigmoid(jnp.dot(xm, wg_ref[:, n * D_MODEL:(n + 1) * D_MODEL], preferred_element_type=F32))
        term = gate * _mm(br, wb_ref[n])
        u = term if u is None else u + term
    out_ref[...] = x + mod_ref[2:3, :] * _mm(u, wo_ref[...])


def _merge(h, mods, norm_w, ys, p_ssd, p_ml, p_gd, p_gl, norms, wg, wb, wo, stream):
    bsz, t, d = h.shape
    grid, tok, mod = _token_grid_specs(bsz, t, stream)
    gate_spec = lambda blk: pl.BlockSpec((None, TOKEN_BLOCK, MIX_W), lambda b, i: (b, i, blk))
    return pl.pallas_call(
        _merge_kernel,
        grid=grid,
        in_specs=([tok(d), mod, _const_spec((1, d))] + [tok(MIX_W)] * 8
                  + [gate_spec(0), gate_spec(2), gate_spec(3), gate_spec(2)]
                  + [_const_spec((1, MIX_W))] * 4
                  + [_const_spec(wg.shape), _const_spec(wb.shape), _const_spec(wo.shape)]),
        out_specs=tok(d),
        out_shape=jax.ShapeDtypeStruct((bsz, t, d), F32),
        compiler_params=_TOKEN_PARAMS,
        name="merge_out",
    )(h, mods, norm_w.reshape(1, d), *ys, p_ssd, p_ml, p_gd, p_gl,
      *[n.reshape(1, MIX_W) for n in norms], wg, wb, wo)


def _ffn_body(h_ref, mod_ref, nw_ref, wg_ref, wu_ref, wd_ref):
    x = h_ref[...]
    xm = _norm_modulate(x, nw_ref[...], mod_ref[3:4, :], mod_ref[4:5, :]).astype(BF16)
    a = jnp.dot(xm, wg_ref[...], preferred_element_type=F32)
    b = jnp.dot(xm, wu_ref[...], preferred_element_type=F32)
    return x + mod_ref[5:6, :] * _mm(_silu(a) * b, wd_ref[...])


def _ffn_kernel(h_ref, mod_ref, nw_ref, wg_ref, wu_ref, wd_ref, out_ref):
    out_ref[...] = _ffn_body(h_ref, mod_ref, nw_ref, wg_ref, wu_ref, wd_ref)


def _ffn_final_kernel(h_ref, mod_ref, nw_ref, wg_ref, wu_ref, wd_ref, nf_ref, out_ref):
    y = _ffn_body(h_ref, mod_ref, nw_ref, wg_ref, wu_ref, wd_ref)
    out_ref[...] = y * lax.rsqrt(jnp.mean(y * y, axis=-1, keepdims=True) + 1e-6) * nf_ref[...]


def _ffn(h, mods, norm_w, wg, wu, wd, stream, final_norm_w=None):
    bsz, t, d = h.shape
    grid, tok, mod = _token_grid_specs(bsz, t, stream)
    extra = [] if final_norm_w is None else [final_norm_w.reshape(1, d)]
    return pl.pallas_call(
        _ffn_kernel if final_norm_w is None else _ffn_final_kernel,
        grid=grid,
        in_specs=[tok(d), mod, _const_spec((1, d)), _const_spec(wg.shape), _const_spec(wu.shape),
                  _const_spec(wd.shape)] + [_const_spec((1, d))] * len(extra),
        out_specs=tok(d),
        out_shape=jax.ShapeDtypeStruct((bsz, t, d), F32),
        compiler_params=_TOKEN_PARAMS,
        name="swiglu",
    )(h, mods, norm_w.reshape(1, d), wg, wu, wd, *extra)


def _pad_cols(w, total):
    return jnp.pad(w, ((0, 0), (0, total - w.shape[1])))


def _lane_row(values, lane0):
    v = values.reshape(-1).astype(F32)
    return jnp.zeros((1, LANES), F32).at[0, lane0:lane0 + v.shape[0]].set(v)


def kernel(x, c, ctx, c_ctx, ada_w, ada_b, norm_mix_w, w_in, ssd_conv_w, ssd_conv_b, ssd_dt_bias,
           ssd_a_log, ssd_d, ssd_norm_w, mlstm_i_bias, mlstm_f_bias, mlstm_norm_w, gdn_conv_w, gdn_a_log,
           gdn_dt_bias, gdn_norm_w, gla_gk_up, gla_gk_bias, gla_norm_w, w_branch, w_out, norm_ffn_w,
           ffn_w_gate, ffn_w_up, ffn_w_down, norm_f_w):
    bsz, seq, d = x.shape
    depth = w_in.shape[0]
    assert d == D_MODEL and ctx.shape[1] % TOKEN_BLOCK == 0 and seq % TOKEN_BLOCK == 0

    cond = jnp.stack([jnp.broadcast_to(c_ctx, c.shape), c], axis=1).reshape(2 * bsz, d)
    mods_all = _mods(cond, ada_w, ada_b).reshape(depth, 2 * bsz, 6, d)

    ssd_cols = 2 * MIX_W + 2 * SSD_GROUPS * SSD_STATE + 2 * SSD_HEADS
    ml_cols = 2 * MLSTM_HEADS * MLSTM_K + 2 * MIX_W + 4 * MLSTM_HEADS
    gd_cols = 4 * MIX_W + 4 * GDN_HEADS
    gl_cols = 2 * GLA_HEADS * GLA_K + 2 * MIX_W + 2 * GLA_RANK
    o1, o2, o3, o4 = ssd_cols, ssd_cols + ml_cols, ssd_cols + ml_cols + gd_cols, ssd_cols + ml_cols + gd_cols + gl_cols

    hs = [ctx, x]
    for l in range(depth):
        last = l == depth - 1
        wl = w_in[l]
        ws = (_pad_cols(wl[:, :o1], SSD_P).astype(BF16), _pad_cols(wl[:, o1:o2], MLSTM_P).astype(BF16),
              _pad_cols(wl[:, o2:o3], GDN_P).astype(BF16), _pad_cols(wl[:, o3:o4], GLA_P).astype(BF16))
        mods = mods_all[l]
        ssd_par = (ssd_conv_w[l], ssd_conv_b[l].reshape(1, -1), _lane_row(ssd_dt_bias[l], 0),
                   _lane_row(-jnp.exp(ssd_a_log[l]), 0), jnp.repeat(ssd_d[l], SSD_HEAD_DIM).reshape(1, MIX_W))
        ml_par = (_lane_row(mlstm_i_bias[l], 0), _lane_row(mlstm_f_bias[l], 2 * MLSTM_HEADS))
        gd_par = (gdn_conv_w[l], _lane_row(gdn_dt_bias[l], 0), _lane_row(-jnp.exp(gdn_a_log[l]), 0))
        up = jnp.zeros((2, LANES, GLA_HEADS * GLA_K), F32)
        for dd in range(2):
            up = up.at[dd, dd * GLA_RANK:(dd + 1) * GLA_RANK].set(gla_gk_up[l, dd])
        gl_par = (up.astype(BF16), gla_gk_bias[l])
        merge_w = (wl[:, o4:].astype(BF16), w_branch[l].astype(BF16), w_out[l].astype(BF16))
        ffn_w = (ffn_w_gate[l].astype(BF16), ffn_w_up[l].astype(BF16), ffn_w_down[l].astype(BF16))
        norms = (ssd_norm_w[l], mlstm_norm_w[l], gdn_norm_w[l], gla_norm_w[l])

        states = [_ssd_state(bsz), _mlstm_state(bsz), _gdn_state(bsz), _gla_state(bsz)]
        for stream in (0, 1):
            h = hs[stream]
            ps = _inproj(h, mods, norm_mix_w[l], ws, stream)
            cm = stream == 1
            ssd_y = _ssd_scan(ps[0], *ssd_par, states[0])
            ml_y = _mlstm_scan(ps[1], *ml_par, states[1], cm)
            gd_y = _gdn_scan(ps[2], *gd_par, states[2], cm)
            gl_y = _gla_scan(ps[3], *gl_par, states[3])
            states = [ssd_y[2], ml_y[2], gd_y[2], gl_y[2]]
            if stream == 0 and last:
                continue
            ys = ssd_y[:2] + ml_y[:2] + gd_y[:2] + gl_y[:2]
            h = _merge(h, mods, norm_mix_w[l], ys, *ps, norms, *merge_w, stream)
            hs[stream] = _ffn(h, mods, norm_ffn_w[l], *ffn_w, stream, norm_f_w if last else None)
    return hs[1]
```

```python
import functools
from typing import NamedTuple

import jax
import jax.numpy as jnp
from jax import lax
from jax.experimental import pallas as pl
from jax.experimental.pallas import tpu as pltpu

F32 = jnp.float32
BF16 = jnp.bfloat16
HIGHEST = lax.Precision.HIGHEST

D_MODEL = 1024
GRID_W = 64
CHUNK = 64
CHUNK_SHIFT = CHUNK.bit_length() - 1
CONV_K = 5
CONV_HALO = 8
MIX_W = 512
LANES = 128
TOKEN_BLOCK = 256
TOKEN_TILE = 512
COLS_PER_BLOCK = 8
FFN_HIDDEN = 2816
VMEM_LIMIT = 56 * 1024 * 1024

SSD_HEADS, SSD_HEAD_DIM, SSD_STATE, SSD_GROUPS = 8, 64, 128, 2
MLSTM_HEADS, MLSTM_K, MLSTM_V = 4, 64, 128
GDN_HEADS, GDN_HEAD_DIM = 4, 128
GLA_HEADS, GLA_K, GLA_V, GLA_RANK, GLA_GATE_NORM = 4, 64, 128, 16, 16

SSD_P = 2 * MIX_W + 2 * SSD_GROUPS * SSD_STATE + LANES
MLSTM_P = 2 * MLSTM_HEADS * MLSTM_K + 2 * MIX_W + LANES
GDN_P = 4 * MIX_W + LANES
GLA_P = 2 * GLA_HEADS * GLA_K + 2 * MIX_W + LANES


class Geo(NamedTuple):
    nblk: int
    tb: int
    colmajor: bool
    c: int


def _mm(a, b):
    return jnp.dot(a.astype(BF16), b.astype(BF16), preferred_element_type=F32)


def _mm_nt(a, b):
    return lax.dot_general(a.astype(BF16), b.astype(BF16), (((1,), (1,)), ((), ())),
                           preferred_element_type=F32)


def _mm_tn(a, b):
    return lax.dot_general(a.astype(BF16), b.astype(BF16), (((0,), (0,)), ((), ())),
                           preferred_element_type=F32)


def _chunk_scan(x, d, op, identity):
    n = x.shape[0]
    pos = _iota2(x.shape, 0) & (CHUNK - 1)
    sh = 1
    while sh < CHUNK:
        if d == 0:
            x = op(x, jnp.where(pos >= sh, pltpu.roll(x, sh, axis=0), identity))
        else:
            x = op(x, jnp.where(pos < CHUNK - sh, pltpu.roll(x, n - sh, axis=0), identity))
        sh *= 2
    return x


def _chunk_cumsum(x, d):
    return _chunk_scan(x, d, jnp.add, 0.0)


def _rows_of(sel, x):
    return lax.dot_general(sel, x, (((1,), (1,)), ((), ())), precision=HIGHEST,
                           preferred_element_type=F32)


def _sigmoid(x):
    return 1.0 / (1.0 + jnp.exp(-x))


def _silu(x):
    return x * _sigmoid(x)


def _softplus(x):
    return jnp.maximum(x, 0.0) + jnp.log1p(jnp.exp(-jnp.abs(x)))


def _log_sigmoid(x):
    return jnp.minimum(x, 0.0) - jnp.log1p(jnp.exp(-jnp.abs(x)))


def _iota2(shape, axis):
    return lax.broadcasted_iota(jnp.int32, shape, axis)


def _one_hot_rows(nrows, lane0):
    r, c = _iota2((nrows, LANES), 0), _iota2((nrows, LANES), 1)
    return jnp.where(c == r + lane0, 1.0, 0.0).astype(F32)


def _chunk_rows(c, d, nch):
    r0 = c * CHUNK if d == 0 else (nch - 1 - c) * CHUNK
    return pl.ds(pl.multiple_of(r0, CHUNK), CHUNK)


def _pair_cols(x, lane_hi, la):
    return jnp.where(lane_hi, x[:, la + 1:la + 2], x[:, la:la + 1])


def _block_diag2(x, lane_hi):
    return jnp.concatenate([jnp.where(lane_hi, 0.0, x), jnp.where(lane_hi, x, 0.0)], axis=0)


def _block_diag_wide(a, b):
    z = jnp.zeros_like(a)
    return jnp.concatenate([jnp.concatenate([a, z], axis=1), jnp.concatenate([z, b], axis=1)], axis=0)


def _group_rmsnorm(x, w, groups, eps=1e-6):
    gw = x.shape[-1] // groups
    outs = []
    for g in range(groups):
        xg = x[:, g * gw:(g + 1) * gw]
        ms = jnp.mean(xg * xg, axis=-1, keepdims=True)
        outs.append(xg * lax.rsqrt(ms + eps))
    return jnp.concatenate(outs, axis=-1) * w


def _norm_modulate(x, nw, shift, scale, eps=1e-6):
    ms = jnp.mean(x * x, axis=-1, keepdims=True)
    y = x * lax.rsqrt(ms + eps) * nw
    return y * (1.0 + scale) + shift


def _stage(dst_ref, row0, p_ref, c0, c1, geo):
    if not geo.colmajor:
        dst_ref[row0:row0 + geo.tb, :] = p_ref[:, c0:c1]
    else:
        for j in range(geo.tb // CHUNK):
            dst_ref[row0 + j * CHUNK:row0 + (j + 1) * CHUNK, :] = p_ref[:, j, c0:c1]


def _unstage(y_ref, ysc_ref, geo):
    for j in range(geo.tb // CHUNK):
        y_ref[:, j, :] = ysc_ref[j * CHUNK:(j + 1) * CHUNK, :]


def _conv_silu(dst_ref, xpad_ref, p_ref, prev_ref, next_ref, col0, width, w_ref, b_ref,
               left_valid, right_valid, geo):
    tb = geo.tb
    base = CONV_HALO - CONV_K // 2
    win = CHUNK + 2 * CONV_HALO
    for s in range(width // LANES):
        c0 = col0 + s * LANES
        ocols = slice(s * LANES, (s + 1) * LANES)
        if geo.colmajor:
            prev = prev_ref[:, COLS_PER_BLOCK - 1, c0 - col0:c0 - col0 + LANES]
            nxt = next_ref[:, 0, c0 - col0:c0 - col0 + LANES]
        else:
            prev, nxt = prev_ref[:, c0:c0 + LANES], next_ref[:, c0:c0 + LANES]
        xpad_ref[0:CONV_HALO, :] = jnp.where(left_valid, prev, 0.0)
        _stage(xpad_ref, CONV_HALO, p_ref, c0, c0 + LANES, geo)
        xpad_ref[CONV_HALO + tb:2 * CONV_HALO + tb, :] = jnp.where(right_valid, nxt, 0.0)
        taps = [w_ref[k:k + 1, ocols] for k in range(CONV_K)]
        bias = None if b_ref is None else b_ref[:, ocols]

        def tile(rt, carry):
            r0 = pl.multiple_of(rt * CHUNK, CHUNK)
            acc = taps[0] * xpad_ref[pl.ds(r0 + base, CHUNK), :]
            for k in range(1, CONV_K):
                acc = acc + taps[k] * xpad_ref[pl.ds(r0 + base + k, CHUNK), :]
            if bias is not None:
                acc = acc + bias
            dst_ref[pl.ds(r0, CHUNK), ocols] = _silu(acc)
            return carry

        lax.fori_loop(0, tb // CHUNK, tile, 0, unroll=2)


def _mods_kernel(c_ref, w_ref, b_ref, o_ref):
    o_ref[...] = _mm(_silu(c_ref[...]), w_ref[...]) + b_ref[...]


def _mods(cond, ada_w, ada_b):
    depth, d, n = ada_w.shape
    r = cond.shape[0]
    nt = n // d
    return pl.pallas_call(
        _mods_kernel,
        grid=(depth, nt),
        in_specs=[pl.BlockSpec((r, d), lambda l, j: (0, 0)),
                  pl.BlockSpec((None, d, d), lambda l, j: (l, 0, j)),
                  pl.BlockSpec((None, 1, d), lambda l, j: (l, 0, j))],
        out_specs=pl.BlockSpec((None, r, d), lambda l, j: (l, 0, j)),
        out_shape=jax.ShapeDtypeStruct((depth, r, n), F32),
        name="adaln_mods",
    )(cond, ada_w, ada_b.reshape(depth, 1, n))


def _token_tile(t):
    return TOKEN_TILE if t % TOKEN_TILE == 0 else TOKEN_BLOCK


def _token_grid_specs(bsz, t, stream):
    tm = _token_tile(t)
    tok = lambda width: pl.BlockSpec((None, tm, width), lambda b, i: (b, i, 0))
    mod = pl.BlockSpec((None, 6, D_MODEL), lambda b, i: (2 * b + stream, 0, 0))
    return (bsz, t // tm), tok, mod


def _const_spec(shape):
    zeros = (0,) * len(shape)
    return pl.BlockSpec(shape, lambda b, i: zeros, pipeline_mode=pl.Buffered(1))


_TOKEN_PARAMS = pltpu.CompilerParams(dimension_semantics=("parallel", "parallel"), vmem_limit_bytes=VMEM_LIMIT)


def _inproj_kernel(h_ref, mod_ref, nw_ref, w0, w1, w2, w3, o0, o1, o2, o3):
    xm = _norm_modulate(h_ref[...], nw_ref[...], mod_ref[0:1, :], mod_ref[1:2, :]).astype(BF16)
    for w, o in ((w0, o0), (w1, o1), (w2, o2), (w3, o3)):
        o[...] = jnp.dot(xm, w[...], preferred_element_type=F32)


def _inproj(h, mods, norm_w, ws, stream):
    bsz, t, d = h.shape
    grid, tok, mod = _token_grid_specs(bsz, t, stream)
    return pl.pallas_call(
        _inproj_kernel,
        grid=grid,
        in_specs=[tok(d), mod, _const_spec((1, d))] + [_const_spec(w.shape) for w in ws],
        out_specs=[tok(w.shape[1]) for w in ws],
        out_shape=[jax.ShapeDtypeStruct((bsz, t, w.shape[1]), F32) for w in ws],
        compiler_params=_TOKEN_PARAMS,
        name="in_proj",
    )(h, mods, norm_w.reshape(1, d), *ws)


def _scan_call(kernel, p, params, states, scratch_fn, colmajor, conv, name):
    bsz, t, c = p.shape
    if colmajor:
        rows = t // GRID_W
        assert rows == CHUNK and GRID_W % COLS_PER_BLOCK == 0
        geo = Geo(GRID_W // COLS_PER_BLOCK, COLS_PER_BLOCK * CHUNK, True, c)
        pv = p.reshape(bsz, rows, GRID_W, c)
        yshape = (bsz, rows, GRID_W, MIX_W)
        last_halo = rows // CONV_HALO - 1

        def specs(blk):
            main = pl.BlockSpec((None, rows, COLS_PER_BLOCK, c), lambda b, i: (b, 0, blk(i), 0))
            if not conv:
                return [main]
            col0, width = conv
            assert col0 % width == 0
            prev = pl.BlockSpec((None, CONV_HALO, COLS_PER_BLOCK, width),
                                lambda b, i: (b, last_halo, jnp.maximum(blk(i) - 1, 0), col0 // width))
            nxt = pl.BlockSpec((None, CONV_HALO, COLS_PER_BLOCK, width),
                               lambda b, i: (b, 0, jnp.minimum(blk(i) + 1, geo.nblk - 1), col0 // width))
            return [main, prev, nxt]

        yspec = lambda blk: pl.BlockSpec((None, rows, COLS_PER_BLOCK, MIX_W), lambda b, i: (b, 0, blk(i), 0))
    else:
        assert t % TOKEN_BLOCK == 0
        geo = Geo(t // TOKEN_BLOCK, TOKEN_BLOCK, False, c)
        pv = p
        yshape = (bsz, t, MIX_W)
        per_halo = TOKEN_BLOCK // CONV_HALO
        nhalo = t // CONV_HALO

        def specs(blk):
            main = pl.BlockSpec((None, TOKEN_BLOCK, c), lambda b, i: (b, blk(i), 0))
            prev = pl.BlockSpec((None, CONV_HALO, c), lambda b, i: (b, jnp.maximum(blk(i) * per_halo - 1, 0), 0))
            nxt = pl.BlockSpec((None, CONV_HALO, c),
                               lambda b, i: (b, jnp.minimum((blk(i) + 1) * per_halo, nhalo - 1), 0))
            return [main, prev, nxt] if conv else [main]

        yspec = lambda blk: pl.BlockSpec((None, TOKEN_BLOCK, MIX_W), lambda b, i: (b, blk(i), 0))

    fwd = lambda i: i
    rev = lambda i: geo.nblk - 1 - i
    n_p = 3 if conv else 1
    state_specs = [pl.BlockSpec((None,) + s.shape[1:], lambda b, i, n=s.ndim - 1: (b,) + (0,) * n) for s in states]
    yout = jax.ShapeDtypeStruct(yshape, F32)
    outs = pl.pallas_call(
        functools.partial(kernel, geo),
        grid=(bsz, geo.nblk),
        in_specs=specs(fwd) + specs(rev) + [_const_spec(w.shape) for w in params] + state_specs,
        out_specs=[yspec(fwd), yspec(rev)] + state_specs,
        out_shape=[yout, yout] + [jax.ShapeDtypeStruct(s.shape, F32) for s in states],
        scratch_shapes=scratch_fn(geo),
        compiler_params=pltpu.CompilerParams(dimension_semantics=("parallel", "arbitrary"),
                                             vmem_limit_bytes=VMEM_LIMIT),
        name=name,
    )(*([pv] * (2 * n_p)), *params, *states)
    return outs[0].reshape(bsz, t, MIX_W), outs[1].reshape(bsz, t, MIX_W), tuple(outs[2:])


def _y_targets(geo, yf, yb, ysc):
    return (ysc.at[0], ysc.at[1]) if geo.colmajor else (yf, yb)


def _y_finish(geo, yf, yb, ysc):
    if geo.colmajor:
        _unstage(yf, ysc.at[0], geo)
        _unstage(yb, ysc.at[1], geo)


def _ssd_kernel(geo, pf, pfp, pfn, pb, pbp, pbn, cw, cb, dtb, nega, dskip, s_in,
                yf, yb, s_ref, xpad, xbc, cum_s, v_s):
    i = pl.program_id(1)
    tb = geo.tb
    nch = tb // CHUNK
    hd = SSD_HEAD_DIM
    assert hd == CHUNK

    @pl.when(i == 0)
    def _():
        s_ref[...] = s_in[...]

    for d, (p, pp, pn) in enumerate(((pf, pfp, pfn), (pb, pbp, pbn))):
        j = i if d == 0 else geo.nblk - 1 - i
        _conv_silu(xbc.at[d], xpad, p, pp, pn, MIX_W, 2 * MIX_W, cw, cb, j != 0, j != geo.nblk - 1, geo)
        delta = _softplus(p[:, 3 * MIX_W:3 * MIX_W + LANES] + dtb[...])
        cum_s[d] = _chunk_cumsum(nega[...] * delta, d)
        for h in range(SSD_HEADS):
            lane = d * SSD_HEADS + h
            v_s[d, :, h * hd:(h + 1) * hd] = xbc[d, :, h * hd:(h + 1) * hd] * delta[:, lane:lane + 1]

    hpg = SSD_HEADS // SSD_GROUPS
    gw = hpg * hd
    groups = [(d, g) for d in (0, 1) for g in range(SSD_GROUPS)]
    blk = _iota2((CHUNK, gw), 1) >> CHUNK_SHIFT
    col = _iota2((CHUNK, gw), 1) & (CHUNK - 1)
    row = _iota2((CHUNK, gw), 0)

    def group_cols(x, base):
        out = x[:, base:base + 1]
        for j in range(1, hpg):
            out = jnp.where(blk[0:x.shape[0]] == j, x[:, base + j:base + j + 1], out)
        return out

    def chunk(c, carry):
        dirs = []
        for d in (0, 1):
            rows = _chunk_rows(c, d, nch)
            cum = cum_s[d, rows, :]
            tot = cum[CHUNK - 1:CHUNK, :] if d == 0 else cum[0:1, :]
            cum_t = _rows_of(_one_hot_rows(SSD_HEADS, d * SSD_HEADS), jnp.concatenate([cum] * hpg, axis=0))
            dirs.append(dict(rows=rows, cum=cum, tot=tot, cum_t=cum_t))

        st = []
        for d, g in groups:
            dd = dirs[d]
            rows = dd["rows"]
            base = d * SSD_HEADS + g * hpg
            incl = (col <= row) if d == 0 else (col >= row)
            ccol = group_cols(dd["cum"], base)
            crow = dd["cum_t"][g * hpg:g * hpg + 1, :]
            for j in range(1, hpg):
                crow = jnp.where(blk[0:1] == j, dd["cum_t"][g * hpg + j:g * hpg + j + 1, :], crow)
            tot = group_cols(dd["tot"], base)
            decay = jnp.exp(jnp.where(incl, ccol - crow, -jnp.inf))
            bg = xbc[d, rows, MIX_W + g * SSD_STATE:MIX_W + (g + 1) * SSD_STATE]
            cg = xbc[d, rows, MIX_W + (SSD_GROUPS + g) * SSD_STATE:MIX_W + (SSD_GROUPS + g + 1) * SSD_STATE]
            vg = v_s[d, rows, g * gw:(g + 1) * gw]
            scores = _mm_nt(cg, jnp.concatenate([bg] * hpg, axis=0))
            inter = _mm(cg, s_ref[d, g])
            v_bd = jnp.concatenate([jnp.where(blk == j, vg, 0.0) for j in range(hpg)], axis=0)
            st.append(dict(p=scores * decay, inter=inter * jnp.exp(ccol), v_bd=v_bd, bg=bg,
                           v_end=vg * jnp.exp(tot - ccol), e_tot=jnp.exp(tot)))

        for (d, g), s in zip(groups, st):
            rows = dirs[d]["rows"]
            y = yf if d == 0 else yb
            yg = _mm(s["p"], s["v_bd"]) + s["inter"]
            if d == 0:
                yg = yg + xbc[d, rows, g * gw:(g + 1) * gw] * dskip[:, g * gw:(g + 1) * gw]
            y[rows, g * gw:(g + 1) * gw] = yg
            s_ref[d, g] = s_ref[d, g] * s["e_tot"] + _mm_tn(s["bg"], s["v_end"])
        return carry

    lax.fori_loop(0, nch, chunk, 0)


def _ssd_scan(p, cw, cb, dtb, nega, dskip, state):
    def scratch(geo):
        return [pltpu.VMEM((geo.tb + 2 * CONV_HALO, LANES), F32),
                pltpu.VMEM((2, geo.tb, 2 * MIX_W), F32),
                pltpu.VMEM((2, geo.tb, LANES), F32),
                pltpu.VMEM((2, geo.tb, MIX_W), F32)]
    return _scan_call(_ssd_kernel, p, (cw, cb, dtb, nega, dskip), state, scratch, False, (MIX_W, 2 * MIX_W), "ssd_scan")


def _ssd_state(bsz):
    return (jnp.zeros((bsz, 2, SSD_GROUPS, SSD_STATE, MIX_W // SSD_GROUPS), F32),)


def _mlstm_kernel(geo, pf, pb, ib, fb, c_in, m_in, yf, yb, c_ref, m_ref, qkv, b_s, li_s, cm_s, ysc):
    i = pl.program_id(1)
    tb = geo.tb
    nch = tb // CHUNK
    nh, dk, dv = MLSTM_HEADS, MLSTM_K, MLSTM_V
    kscale = dk ** -0.5
    k0, v0, g0 = nh * dk, 2 * nh * dk, 2 * nh * dk + 2 * MIX_W
    fl = 2 * nh

    @pl.when(i == 0)
    def _():
        c_ref[...] = c_in[...]
        m_ref[...] = m_in[...]

    for d, p in enumerate((pf, pb)):
        _stage(qkv.at[d], 0, p, 0, g0 - MIX_W, geo)
        _stage(li_s.at[d], 0, p, g0, g0 + LANES, geo)
        gates = li_s[d]
        li = pltpu.roll(gates + ib[...], fl, axis=1)
        b = _chunk_cumsum(_log_sigmoid(gates + fb[...]), d)
        li_s[d] = li
        b_s[d] = b
        cm_s[d] = _chunk_scan(li - b, d, jnp.maximum, -jnp.inf)

    ones_col = jnp.where(_iota2((CHUNK, dv), 1) == 0, 1.0, 0.0).astype(F32)
    npair = nh // 2
    pairs = [(d, hp) for d in (0, 1) for hp in range(npair)]
    lane_hi = _iota2((CHUNK, 2 * CHUNK), 1) >= CHUNK
    col = _iota2((CHUNK, 2 * CHUNK), 1) & (CHUNK - 1)
    row = _iota2((CHUNK, 2 * CHUNK), 0)
    ew = 2 * dv
    ydst = _y_targets(geo, yf, yb, ysc)

    def chunk(c, carry):
        dirs = []
        for d in (0, 1):
            rows = _chunk_rows(c, d, nch)
            sel = _one_hot_rows(8, fl + d * nh)
            b = b_s[d, rows, :]
            li = li_s[d, rows, :]
            tot = b[CHUNK - 1:CHUNK, :] if d == 0 else b[0:1, :]
            m_row = m_ref[d:d + 1, :]
            end_log = tot - b + li
            m_new = jnp.maximum(tot + m_row, jnp.max(end_log, axis=0, keepdims=True))
            m_ref[d:d + 1, :] = m_new
            inter = b + m_row
            bli = b - li
            dirs.append(dict(rows=rows, b=b, inter=inter, m_t=jnp.maximum(inter, b + cm_s[d, rows, :]),
                             bli_t=_rows_of(sel, jnp.concatenate([bli, bli], axis=0)),
                             w_end=jnp.exp(end_log - m_new), keep=jnp.exp(tot + m_row - m_new)))

        st = []
        for d, hp in pairs:
            g = dirs[d]
            la = fl + d * nh + 2 * hp
            incl = (col <= row) if d == 0 else (col >= row)
            bli_row = jnp.where(lane_hi[0:1], g["bli_t"][2 * hp + 1:2 * hp + 2, :], g["bli_t"][2 * hp:2 * hp + 1, :])
            d_log = jnp.where(incl, _pair_cols(g["b"], lane_hi, la) - bli_row, -jnp.inf)
            m_pair = _pair_cols(g["m_t"], lane_hi, la)
            q = qkv[d, g["rows"], hp * 2 * dk:(hp + 1) * 2 * dk]
            k = qkv[d, g["rows"], k0 + hp * 2 * dk:k0 + (hp + 1) * 2 * dk] * kscale
            vext = [jnp.concatenate([qkv[d, g["rows"], v0 + (2 * hp + h) * dv:v0 + (2 * hp + h + 1) * dv], ones_col],
                                    axis=1) for h in (0, 1)]
            scores = _mm_nt(q, _block_diag2(k, lane_hi)) * jnp.exp(d_log - m_pair)
            qw = q * jnp.exp(_pair_cols(g["inter"], lane_hi, la) - m_pair)
            st.append(dict(lhs=jnp.concatenate([scores, qw], axis=1), k=k, vext=vext))

        res = [_mm(s["lhs"], jnp.concatenate([_block_diag_wide(s["vext"][0], s["vext"][1]), c_ref[d, hp]], axis=0))
               for (d, hp), s in zip(pairs, st)]

        for (d, hp), s, r in zip(pairs, st, res):
            g = dirs[d]
            la = fl + d * nh + 2 * hp
            kw = s["k"] * _pair_cols(g["w_end"], lane_hi, la)
            for h in (0, 1):
                den = jnp.maximum(jnp.abs(r[:, h * ew + dv:h * ew + dv + 1]), jnp.exp(-g["m_t"][:, la + h:la + h + 1]))
                ydst[d][g["rows"], (2 * hp + h) * dv:(2 * hp + h + 1) * dv] = r[:, h * ew:h * ew + dv] / den
                kw_h = jnp.where(lane_hi, kw, 0.0) if h == 1 else jnp.where(lane_hi, 0.0, kw)
                cs = slice(h * ew, (h + 1) * ew)
                c_ref[d, hp, :, cs] = c_ref[d, hp, :, cs] * g["keep"][:, la + h:la + h + 1] + _mm_tn(kw_h, s["vext"][h])
        return carry

    lax.fori_loop(0, nch, chunk, 0)
    _y_finish(geo, yf, yb, ysc)


def _mlstm_scan(p, ib, fb, state, colmajor):
    def scratch(geo):
        return [pltpu.VMEM((2, geo.tb, 2 * MLSTM_HEADS * MLSTM_K + MIX_W), F32),
                pltpu.VMEM((2, geo.tb, LANES), F32),
                pltpu.VMEM((2, geo.tb, LANES), F32),
                pltpu.VMEM((2, geo.tb, LANES), F32),
                pltpu.VMEM((2, geo.tb, MIX_W) if geo.colmajor else (2, 8, LANES), F32)]
    return _scan_call(_mlstm_kernel, p, (ib, fb), state, scratch, colmajor, None, "mlstm_scan")


def _mlstm_state(bsz):
    return (jnp.zeros((bsz, 2, MLSTM_HEADS // 2, 2 * MLSTM_K, 4 * MLSTM_V), F32), jnp.zeros((bsz, 8, LANES), F32))


GDN_PREP_CHUNKS = 4


def _gdn_kernel(geo, pf, pfp, pfn, pb, pbp, pbn, cw, dtb, nega, s_in, yf, yb,
                s_ref, xpad, qkv, cum_s, beta_s, lvl, sol_s, attn_s, ysc):
    i = pl.program_id(1)
    tb = geo.tb
    nch = tb // CHUNK
    nh, hd = GDN_HEADS, GDN_HEAD_DIM
    g0 = 4 * MIX_W
    npair = nh // 2
    pw = 2 * hd

    @pl.when(i == 0)
    def _():
        s_ref[...] = s_in[...]
        r, c = _iota2((CHUNK, 2 * CHUNK), 0), _iota2((CHUNK, 2 * CHUNK), 1) & (CHUNK - 1)
        for k in range(lvl.shape[0]):
            joined = ((r >> (k + 1)) == (c >> (k + 1))) & (((r >> k) & 1) != ((c >> k) & 1))
            lvl[k] = jnp.where(joined, 1.0, 0.0).astype(F32)

    for d, (p, pp, pn) in enumerate(((pf, pfp, pfn), (pb, pbp, pbn))):
        j = i if d == 0 else geo.nblk - 1 - i
        _conv_silu(qkv.at[d], xpad, p, pp, pn, 0, 3 * MIX_W, cw, None, j != 0, j != geo.nblk - 1, geo)
        for h in range(2 * nh):
            hs = slice(h * hd, (h + 1) * hd)
            x = qkv[d, :, hs]
            x = x * lax.rsqrt(jnp.sum(x * x, axis=-1, keepdims=True) + 1e-6)
            qkv[d, :, hs] = x * (hd ** -0.5) if h < nh else x
        _stage(beta_s.at[d], 0, p, g0, g0 + LANES, geo)
        gates = beta_s[d]
        cum_s[d] = _chunk_cumsum(nega[...] * _softplus(gates + dtb[...]), d)
        beta_s[d] = pltpu.roll(_sigmoid(gates), LANES - 2 * nh, axis=1)

    lane_hi = _iota2((CHUNK, 2 * CHUNK), 1) >= CHUNK
    col = _iota2((CHUNK, 2 * CHUNK), 1) & (CHUNK - 1)
    row = _iota2((CHUNK, 2 * CHUNK), 0)
    ydst = _y_targets(geo, yf, yb, ysc)

    def gate_cols(c, d):
        rows = _chunk_rows(c, d, nch)
        cum = cum_s[d, rows, :]
        tot = cum[CHUNK - 1:CHUNK, :] if d == 0 else cum[0:1, :]
        return rows, cum, tot

    def heads(d, rows, base, hp):
        return [qkv[d, rows, base + (2 * hp + h) * hd:base + (2 * hp + h + 1) * hd] for h in (0, 1)]

    def prep(it, carry):
        dirs = {}
        for cc in range(GDN_PREP_CHUNKS):
            for d in (0, 1):
                rows, cum, tot = gate_cols(it * GDN_PREP_CHUNKS + cc, d)
                cum_t = _rows_of(_one_hot_rows(8, d * nh), jnp.concatenate([cum, cum], axis=0))
                dirs[cc, d] = dict(rows=rows, cum=cum, beta=beta_s[d, rows, :], cum_t=cum_t, e_cum=jnp.exp(cum))
        units = [(cc, d, hp) for cc in range(GDN_PREP_CHUNKS) for d in (0, 1) for hp in range(npair)]

        st = []
        for cc, d, hp in units:
            g = dirs[cc, d]
            la = d * nh + 2 * hp
            incl = (col <= row) if d == 0 else (col >= row)
            strict = (col < row) if d == 0 else (col > row)
            crow = jnp.where(lane_hi[0:1], g["cum_t"][2 * hp + 1:2 * hp + 2, :], g["cum_t"][2 * hp:2 * hp + 1, :])
            decay = jnp.exp(jnp.where(incl, _pair_cols(g["cum"], lane_hi, la) - crow, -jnp.inf))
            q, k, v = (heads(d, g["rows"], base, hp) for base in (0, MIX_W, 2 * MIX_W))
            bcol = [g["beta"][:, la + h:la + h + 1] for h in (0, 1)]
            ecol = [g["e_cum"][:, la + h:la + h + 1] for h in (0, 1)]
            kb = [k[h] * bcol[h] for h in (0, 1)]
            lhs = jnp.concatenate([jnp.concatenate(kb, axis=1), jnp.concatenate(q, axis=1)], axis=0)
            prod = _mm_nt(lhs, _block_diag_wide(k[0], k[1]))
            attn_s[d, hp, g["rows"], :] = prod[CHUNK:2 * CHUNK] * decay
            rhs = [jnp.concatenate([kb[h] * ecol[h], v[h] * bcol[h]], axis=1) for h in (0, 1)]
            st.append(dict(a=jnp.where(strict, prod[0:CHUNK] * decay, 0.0), rhs=rhs))

        es = [-(s["a"] * lvl[0]) for s in st]
        for lv in range(1, lvl.shape[0]):
            aks = [s["a"] * lvl[lv] for s in st]
            xs = [ak + _mm(ak, _block_diag2(e, lane_hi)) for ak, e in zip(aks, es)]
            es = [e - (x + _mm(e, _block_diag2(x, lane_hi))) for e, x in zip(es, xs)]

        for (cc, d, hp), s, e in zip(units, st, es):
            rhs = s["rhs"]
            sol_s[d, hp, dirs[cc, d]["rows"], :] = (jnp.concatenate(rhs, axis=1)
                                                   + _mm(e, _block_diag_wide(rhs[0], rhs[1])))
        return carry

    lax.fori_loop(0, nch // GDN_PREP_CHUNKS, prep, 0)

    pairs = [(d, hp) for d in (0, 1) for hp in range(npair)]

    def chunk(c, carry):
        dirs = []
        for d in (0, 1):
            rows, cum, tot = gate_cols(c, d)
            dirs.append(dict(rows=rows, e_cum=jnp.exp(cum), e_end=jnp.exp(tot - cum), e_tot=jnp.exp(tot)))
        rs, sols, ks = [], [], []
        for d, hp in pairs:
            g = dirs[d]
            la = d * nh + 2 * hp
            sol = sol_s[d, hp, g["rows"], :]
            q = heads(d, g["rows"], 0, hp)
            qe = jnp.concatenate([q[h] * g["e_cum"][:, la + h:la + h + 1] for h in (0, 1)], axis=1)
            w = jnp.concatenate([sol[:, 0:hd], sol[:, 2 * hd:3 * hd]], axis=1)
            rs.append(_mm(jnp.concatenate([w, qe], axis=0), s_ref[d, hp]))
            sols.append(sol)
        for (d, hp), sol, r in zip(pairs, sols, rs):
            g = dirs[d]
            la = d * nh + 2 * hp
            k = heads(d, g["rows"], MIX_W, hp)
            v_new = [sol[:, hd:2 * hd] - r[0:CHUNK, 0:hd], sol[:, 3 * hd:4 * hd] - r[0:CHUNK, hd:pw]]
            ydst[d][g["rows"], hp * pw:(hp + 1) * pw] = (
                r[CHUNK:2 * CHUNK] + _mm(attn_s[d, hp, g["rows"], :], _block_diag_wide(v_new[0], v_new[1])))
            for h in (0, 1):
                blk = slice(h * hd, (h + 1) * hd)
                k_end = k[h] * g["e_end"][:, la + h:la + h + 1]
                s_ref[d, hp, blk, blk] = (s_ref[d, hp, blk, blk] * g["e_tot"][:, la + h:la + h + 1]
                                          + _mm_tn(k_end, v_new[h]))
        return carry

    lax.fori_loop(0, nch, chunk, 0)
    _y_finish(geo, yf, yb, ysc)


def _gdn_scan(p, cw, dtb, nega, state, colmajor):
    nlevels = CHUNK.bit_length() - 1
    pw = 2 * GDN_HEAD_DIM

    def scratch(geo):
        return [pltpu.VMEM((geo.tb + 2 * CONV_HALO, LANES), F32),
                pltpu.VMEM((2, geo.tb, 3 * MIX_W), F32),
                pltpu.VMEM((2, geo.tb, LANES), F32),
                pltpu.VMEM((2, geo.tb, LANES), F32),
                pltpu.VMEM((nlevels, CHUNK, 2 * CHUNK), F32),
                pltpu.VMEM((2, GDN_HEADS // 2, geo.tb, 2 * pw), F32),
                pltpu.VMEM((2, GDN_HEADS // 2, geo.tb, 2 * CHUNK), F32),
                pltpu.VMEM((2, geo.tb, MIX_W) if geo.colmajor else (2, 8, LANES), F32)]
    return _scan_call(_gdn_kernel, p, (cw, dtb, nega), state, scratch, colmajor, (0, 3 * MIX_W), "gdn_scan")


def _gdn_state(bsz):
    return (jnp.zeros((bsz, 2, GDN_HEADS // 2, 2 * GDN_HEAD_DIM, 2 * GDN_HEAD_DIM), F32),)


def _gla_kernel(geo, pf, pb, up, gb, st_in, yf, yb, st_ref, cum_s):
    i = pl.program_id(1)
    tb = geo.tb
    nch = tb // CHUNK
    nh, dk, dv = GLA_HEADS, GLA_K, GLA_V
    qscale = dk ** -0.5
    q0, k0, v0, g0 = 0, nh * dk, 2 * nh * dk, 2 * nh * dk + 2 * MIX_W
    mid = CHUNK // 2

    @pl.when(i == 0)
    def _():
        st_ref[...] = st_in[...]

    for d, p in enumerate((pf, pb)):
        gk = _mm(p[:, g0:g0 + LANES], up[d]) + gb[d:d + 1, :]
        cum_s[d] = _chunk_cumsum(_log_sigmoid(gk) * (1.0 / GLA_GATE_NORM), d)

    npair = nh // 2
    pairs = [(d, hp) for d in (0, 1) for hp in range(npair)]
    lane_hi = _iota2((CHUNK, 2 * CHUNK), 1) >= CHUNK
    col = _iota2((CHUNK, 2 * CHUNK), 1) & (CHUNK - 1)
    row = _iota2((CHUNK, 2 * CHUNK), 0)
    assert dk == CHUNK

    def chunk(c, carry):
        st = []
        for d, hp in pairs:
            p = pf if d == 0 else pb
            rows = _chunk_rows(c, d, nch)
            ps = slice(hp * 2 * dk, (hp + 1) * 2 * dk)
            cum = cum_s[d, rows, ps]
            ref = cum[mid:mid + 1, :] if d == 0 else cum[CHUNK - 1 - mid:CHUNK - mid, :]
            tot = cum[CHUNK - 1:CHUNK, :] if d == 0 else cum[0:1, :]
            q = p[rows, q0 + hp * 2 * dk:q0 + (hp + 1) * 2 * dk] * qscale
            k = p[rows, k0 + hp * 2 * dk:k0 + (hp + 1) * 2 * dk]
            v = [p[rows, v0 + (2 * hp + h) * dv:v0 + (2 * hp + h + 1) * dv] for h in (0, 1)]
            incl = (col <= row) if d == 0 else (col >= row)
            kg_bd = _block_diag2(k * jnp.exp(ref - cum), lane_hi)
            scores = jnp.where(incl, _mm_nt(q * jnp.exp(cum - ref), kg_bd), 0.0)
            st.append(dict(rows=rows, scores=scores, v=v, qe=q * jnp.exp(cum), ke=k * jnp.exp(tot - cum),
                           e_tot=jnp.exp(tot)))

        for (d, hp), s in zip(pairs, st):
            y = yf if d == 0 else yb
            y[s["rows"], hp * 2 * dv:(hp + 1) * 2 * dv] = (_mm(s["scores"], _block_diag_wide(s["v"][0], s["v"][1]))
                                                          + _mm_nt(s["qe"], st_ref[d, hp]))
            for h in (0, 1):
                ke_h = jnp.where(lane_hi, s["ke"], 0.0) if h == 1 else jnp.where(lane_hi, 0.0, s["ke"])
                rs = slice(h * dv, (h + 1) * dv)
                st_ref[d, hp, rs, :] = st_ref[d, hp, rs, :] * s["e_tot"] + _mm_tn(s["v"][h], ke_h)
        return carry

    lax.fori_loop(0, nch, chunk, 0)


def _gla_scan(p, up, gb, state):
    def scratch(geo):
        return [pltpu.VMEM((2, geo.tb, GLA_HEADS * GLA_K), F32)]
    return _scan_call(_gla_kernel, p, (up, gb), state, scratch, False, None, "gla_scan")


def _gla_state(bsz):
    return (jnp.zeros((bsz, 2, GLA_HEADS // 2, 2 * GLA_V, 2 * GLA_K), F32),)


def _merge_kernel(h_ref, mod_ref, nw_ref,
                  ssd_f, ssd_b, ml_f, ml_b, gd_f, gd_b, gl_f, gl_b,
                  z_ref, o_ref, gg_ref, lg_ref,
                  nssd, nml, ngd, ngl, wg_ref, wb_ref, wo_ref, out_ref):
    x = h_ref[...]
    xm = _norm_modulate(x, nw_ref[...], mod_ref[0:1, :], mod_ref[1:2, :]).astype(BF16)
    branches = (
        _group_rmsnorm((ssd_f[...] + ssd_b[...]) * _silu(z_ref[...]), nssd[...], SSD_GROUPS),
        _group_rmsnorm(ml_f[...] + ml_b[...], nml[...], MLSTM_HEADS) * _sigmoid(o_ref[...]),
        _group_rmsnorm(gd_f[...] + gd_b[...], ngd[...], GDN_HEADS) * _silu(gg_ref[...]),
        _group_rmsnorm(gl_f[...] + gl_b[...], ngl[...], GLA_HEADS) * _silu(lg_ref[...]),
    )
    u = None
    for n, br in enumerate(branches):
        gate = _sigmoid(jnp.dot(xm, wg_ref[:, n * D_MODEL:(n + 1) * D_MODEL], preferred_element_type=F32))
        term = gate * _mm(br, wb_ref[n])
        u = term if u is None else u + term
    out_ref[...] = x + mod_ref[2:3, :] * _mm(u, wo_ref[...])


def _merge(h, mods, norm_w, ys, p_ssd, p_ml, p_gd, p_gl, norms, wg, wb, wo, stream):
    bsz, t, d = h.shape
    grid, tok, mod = _token_grid_specs(bsz, t, stream)
    gate_spec = lambda blk: pl.BlockSpec((None, _token_tile(t), MIX_W), lambda b, i: (b, i, blk))
    return pl.pallas_call(
        _merge_kernel,
        grid=grid,
        in_specs=([tok(d), mod, _const_spec((1, d))] + [tok(MIX_W)] * 8
                  + [gate_spec(0), gate_spec(2), gate_spec(3), gate_spec(2)]
                  + [_const_spec((1, MIX_W))] * 4
                  + [_const_spec(wg.shape), _const_spec(wb.shape), _const_spec(wo.shape)]),
        out_specs=tok(d),
        out_shape=jax.ShapeDtypeStruct((bsz, t, d), F32),
        compiler_params=_TOKEN_PARAMS,
        name="merge_out",
    )(h, mods, norm_w.reshape(1, d), *ys, p_ssd, p_ml, p_gd, p_gl,
      *[n.reshape(1, MIX_W) for n in norms], wg, wb, wo)


def _ffn_body(h_ref, mod_ref, nw_ref, wg_ref, wu_ref, wd_ref):
    x = h_ref[...]
    xm = _norm_modulate(x, nw_ref[...], mod_ref[3:4, :], mod_ref[4:5, :]).astype(BF16)
    a = jnp.dot(xm, wg_ref[...], preferred_element_type=F32)
    b = jnp.dot(xm, wu_ref[...], preferred_element_type=F32)
    return x + mod_ref[5:6, :] * _mm(_silu(a) * b, wd_ref[...])


def _ffn_kernel(h_ref, mod_ref, nw_ref, wg_ref, wu_ref, wd_ref, out_ref):
    out_ref[...] = _ffn_body(h_ref, mod_ref, nw_ref, wg_ref, wu_ref, wd_ref)


def _ffn_final_kernel(h_ref, mod_ref, nw_ref, wg_ref, wu_ref, wd_ref, nf_ref, out_ref):
    y = _ffn_body(h_ref, mod_ref, nw_ref, wg_ref, wu_ref, wd_ref)
    out_ref[...] = y * lax.rsqrt(jnp.mean(y * y, axis=-1, keepdims=True) + 1e-6) * nf_ref[...]


def _ffn(h, mods, norm_w, wg, wu, wd, stream, final_norm_w=None):
    bsz, t, d = h.shape
    grid, tok, mod = _token_grid_specs(bsz, t, stream)
    extra = [] if final_norm_w is None else [final_norm_w.reshape(1, d)]
    return pl.pallas_call(
        _ffn_kernel if final_norm_w is None else _ffn_final_kernel,
        grid=grid,
        in_specs=[tok(d), mod, _const_spec((1, d)), _const_spec(wg.shape), _const_spec(wu.shape),
                  _const_spec(wd.shape)] + [_const_spec((1, d))] * len(extra),
        out_specs=tok(d),
        out_shape=jax.ShapeDtypeStruct((bsz, t, d), F32),
        compiler_params=_TOKEN_PARAMS,
        name="swiglu",
    )(h, mods, norm_w.reshape(1, d), wg, wu, wd, *extra)


def _pad_cols(w, total):
    return jnp.pad(w, ((0, 0), (0, total - w.shape[1])))


def _lane_row(values, lane0):
    v = values.reshape(-1).astype(F32)
    return jnp.zeros((1, LANES), F32).at[0, lane0:lane0 + v.shape[0]].set(v)


def kernel(x, c, ctx, c_ctx, ada_w, ada_b, norm_mix_w, w_in, ssd_conv_w, ssd_conv_b, ssd_dt_bias,
           ssd_a_log, ssd_d, ssd_norm_w, mlstm_i_bias, mlstm_f_bias, mlstm_norm_w, gdn_conv_w, gdn_a_log,
           gdn_dt_bias, gdn_norm_w, gla_gk_up, gla_gk_bias, gla_norm_w, w_branch, w_out, norm_ffn_w,
           ffn_w_gate, ffn_w_up, ffn_w_down, norm_f_w):
    bsz, seq, d = x.shape
    depth = w_in.shape[0]
    assert d == D_MODEL and ctx.shape[1] % TOKEN_BLOCK == 0 and seq % TOKEN_BLOCK == 0

    cond = jnp.stack([jnp.broadcast_to(c_ctx, c.shape), c], axis=1).reshape(2 * bsz, d)
    mods_all = _mods(cond, ada_w, ada_b).reshape(depth, 2 * bsz, 6, d)

    ssd_cols = 2 * MIX_W + 2 * SSD_GROUPS * SSD_STATE + 2 * SSD_HEADS
    ml_cols = 2 * MLSTM_HEADS * MLSTM_K + 2 * MIX_W + 4 * MLSTM_HEADS
    gd_cols = 4 * MIX_W + 4 * GDN_HEADS
    gl_cols = 2 * GLA_HEADS * GLA_K + 2 * MIX_W + 2 * GLA_RANK
    o1, o2, o3, o4 = ssd_cols, ssd_cols + ml_cols, ssd_cols + ml_cols + gd_cols, ssd_cols + ml_cols + gd_cols + gl_cols

    hs = [ctx, x]
    for l in range(depth):
        last = l == depth - 1
        wl = w_in[l]
        ws = (_pad_cols(wl[:, :o1], SSD_P).astype(BF16), _pad_cols(wl[:, o1:o2], MLSTM_P).astype(BF16),
              _pad_cols(wl[:, o2:o3], GDN_P).astype(BF16), _pad_cols(wl[:, o3:o4], GLA_P).astype(BF16))
        mods = mods_all[l]
        ssd_par = (ssd_conv_w[l], ssd_conv_b[l].reshape(1, -1), _lane_row(ssd_dt_bias[l], 0),
                   _lane_row(-jnp.exp(ssd_a_log[l]), 0), jnp.repeat(ssd_d[l], SSD_HEAD_DIM).reshape(1, MIX_W))
        ml_par = (_lane_row(mlstm_i_bias[l], 0), _lane_row(mlstm_f_bias[l], 2 * MLSTM_HEADS))
        gd_par = (gdn_conv_w[l], _lane_row(gdn_dt_bias[l], 0), _lane_row(-jnp.exp(gdn_a_log[l]), 0))
        up = jnp.zeros((2, LANES, GLA_HEADS * GLA_K), F32)
        for dd in range(2):
            up = up.at[dd, dd * GLA_RANK:(dd + 1) * GLA_RANK].set(gla_gk_up[l, dd])
        gl_par = (up.astype(BF16), gla_gk_bias[l])
        merge_w = (wl[:, o4:].astype(BF16), w_branch[l].astype(BF16), w_out[l].astype(BF16))
        ffn_w = (ffn_w_gate[l].astype(BF16), ffn_w_up[l].astype(BF16), ffn_w_down[l].astype(BF16))
        norms = (ssd_norm_w[l], mlstm_norm_w[l], gdn_norm_w[l], gla_norm_w[l])

        states = [_ssd_state(bsz), _mlstm_state(bsz), _gdn_state(bsz), _gla_state(bsz)]
        for stream in (0, 1):
            h = hs[stream]
            ps = _inproj(h, mods, norm_mix_w[l], ws, stream)
            cm = stream == 1
            ssd_y = _ssd_scan(ps[0], *ssd_par, states[0])
            ml_y = _mlstm_scan(ps[1], *ml_par, states[1], cm)
            gd_y = _gdn_scan(ps[2], *gd_par, states[2], cm)
            gl_y = _gla_scan(ps[3], *gl_par, states[3])
            states = [ssd_y[2], ml_y[2], gd_y[2], gl_y[2]]
            if stream == 0 and last:
                continue
            ys = ssd_y[:2] + ml_y[:2] + gd_y[:2] + gl_y[:2]
            h = _merge(h, mods, norm_mix_w[l], ys, *ps, norms, *merge_w, stream)
            hs[stream] = _ffn(h, mods, norm_ffn_w[l], *ffn_w, stream, norm_f_w if last else None)
    return hs[1]
```

```python
import functools
from typing import NamedTuple

import jax
import jax.numpy as jnp
from jax import lax
from jax.experimental import pallas as pl
from jax.experimental.pallas import tpu as pltpu

F32 = jnp.float32
BF16 = jnp.bfloat16
HIGHEST = lax.Precision.HIGHEST

D_MODEL = 1024
GRID_W = 64
CHUNK = 64
CHUNK_SHIFT = CHUNK.bit_length() - 1
CONV_K = 5
CONV_HALO = 8
MIX_W = 512
LANES = 128
TOKEN_BLOCK = 256
TOKEN_TILE = 512
COLS_PER_BLOCK = 8
FFN_HIDDEN = 2816
VMEM_LIMIT = 56 * 1024 * 1024

SSD_HEADS, SSD_HEAD_DIM, SSD_STATE, SSD_GROUPS = 8, 64, 128, 2
MLSTM_HEADS, MLSTM_K, MLSTM_V = 4, 64, 128
GDN_HEADS, GDN_HEAD_DIM = 4, 128
GLA_HEADS, GLA_K, GLA_V, GLA_RANK, GLA_GATE_NORM = 4, 64, 128, 16, 16

SSD_P = 2 * MIX_W + 2 * SSD_GROUPS * SSD_STATE + LANES
MLSTM_P = 2 * MLSTM_HEADS * MLSTM_K + 2 * MIX_W + LANES
GDN_P = 4 * MIX_W + LANES
GLA_P = 2 * GLA_HEADS * GLA_K + 2 * MIX_W + LANES


class Geo(NamedTuple):
    nblk: int
    tb: int
    colmajor: bool
    c: int


def _mm(a, b):
    return jnp.dot(a.astype(BF16), b.astype(BF16), preferred_element_type=F32)


def _mm_nt(a, b):
    return lax.dot_general(a.astype(BF16), b.astype(BF16), (((1,), (1,)), ((), ())),
                           preferred_element_type=F32)


def _mm_tn(a, b):
    return lax.dot_general(a.astype(BF16), b.astype(BF16), (((0,), (0,)), ((), ())),
                           preferred_element_type=F32)


def _chunk_scan(x, d, op, identity):
    n = x.shape[0]
    pos = _iota2(x.shape, 0) & (CHUNK - 1)
    sh = 1
    while sh < CHUNK:
        if d == 0:
            x = op(x, jnp.where(pos >= sh, pltpu.roll(x, sh, axis=0), identity))
        else:
            x = op(x, jnp.where(pos < CHUNK - sh, pltpu.roll(x, n - sh, axis=0), identity))
        sh *= 2
    return x


def _chunk_cumsum(x, d):
    return _chunk_scan(x, d, jnp.add, 0.0)


def _rows_of(sel, x):
    return lax.dot_general(sel, x, (((1,), (1,)), ((), ())), precision=HIGHEST,
                           preferred_element_type=F32)


def _sigmoid(x):
    return 1.0 / (1.0 + jnp.exp(-x))


def _silu(x):
    return x * _sigmoid(x)


def _softplus(x):
    return jnp.maximum(x, 0.0) + jnp.log1p(jnp.exp(-jnp.abs(x)))


def _log_sigmoid(x):
    return jnp.minimum(x, 0.0) - jnp.log1p(jnp.exp(-jnp.abs(x)))


def _iota2(shape, axis):
    return lax.broadcasted_iota(jnp.int32, shape, axis)


def _one_hot_rows(nrows, lane0):
    r, c = _iota2((nrows, LANES), 0), _iota2((nrows, LANES), 1)
    return jnp.where(c == r + lane0, 1.0, 0.0).astype(F32)


def _chunk_rows(c, d, nch):
    r0 = c * CHUNK if d == 0 else (nch - 1 - c) * CHUNK
    return pl.ds(pl.multiple_of(r0, CHUNK), CHUNK)


def _pair_cols(x, lane_hi, la):
    return jnp.where(lane_hi, x[:, la + 1:la + 2], x[:, la:la + 1])


def _block_diag2(x, lane_hi):
    return jnp.concatenate([jnp.where(lane_hi, 0.0, x), jnp.where(lane_hi, x, 0.0)], axis=0)


def _block_diag_wide(a, b):
    z = jnp.zeros_like(a)
    return jnp.concatenate([jnp.concatenate([a, z], axis=1), jnp.concatenate([z, b], axis=1)], axis=0)


def _group_rmsnorm(x, w, groups, eps=1e-6):
    gw = x.shape[-1] // groups
    outs = []
    for g in range(groups):
        xg = x[:, g * gw:(g + 1) * gw]
        ms = jnp.mean(xg * xg, axis=-1, keepdims=True)
        outs.append(xg * lax.rsqrt(ms + eps))
    return jnp.concatenate(outs, axis=-1) * w


def _norm_modulate(x, nw, shift, scale, eps=1e-6):
    ms = jnp.mean(x * x, axis=-1, keepdims=True)
    y = x * lax.rsqrt(ms + eps) * nw
    return y * (1.0 + scale) + shift


def _stage(dst_ref, row0, p_ref, c0, c1, geo):
    if not geo.colmajor:
        dst_ref[row0:row0 + geo.tb, :] = p_ref[:, c0:c1]
    else:
        for j in range(geo.tb // CHUNK):
            dst_ref[row0 + j * CHUNK:row0 + (j + 1) * CHUNK, :] = p_ref[:, j, c0:c1]


def _unstage(y_ref, ysc_ref, geo):
    for j in range(geo.tb // CHUNK):
        y_ref[:, j, :] = ysc_ref[j * CHUNK:(j + 1) * CHUNK, :]


def _conv_silu(dst_ref, xpad_ref, p_ref, prev_ref, next_ref, col0, width, w_ref, b_ref,
               left_valid, right_valid, geo):
    tb = geo.tb
    base = CONV_HALO - CONV_K // 2
    win = CHUNK + 2 * CONV_HALO
    for s in range(width // LANES):
        c0 = col0 + s * LANES
        ocols = slice(s * LANES, (s + 1) * LANES)
        if geo.colmajor:
            prev = prev_ref[:, COLS_PER_BLOCK - 1, c0 - col0:c0 - col0 + LANES]
            nxt = next_ref[:, 0, c0 - col0:c0 - col0 + LANES]
        else:
            prev, nxt = prev_ref[:, c0:c0 + LANES], next_ref[:, c0:c0 + LANES]
        xpad_ref[0:CONV_HALO, :] = jnp.where(left_valid, prev, 0.0)
        _stage(xpad_ref, CONV_HALO, p_ref, c0, c0 + LANES, geo)
        xpad_ref[CONV_HALO + tb:2 * CONV_HALO + tb, :] = jnp.where(right_valid, nxt, 0.0)
        taps = [w_ref[k:k + 1, ocols] for k in range(CONV_K)]
        bias = None if b_ref is None else b_ref[:, ocols]

        def tile(rt, carry):
            r0 = pl.multiple_of(rt * CHUNK, CHUNK)
            acc = taps[0] * xpad_ref[pl.ds(r0 + base, CHUNK), :]
            for k in range(1, CONV_K):
                acc = acc + taps[k] * xpad_ref[pl.ds(r0 + base + k, CHUNK), :]
            if bias is not None:
                acc = acc + bias
            dst_ref[pl.ds(r0, CHUNK), ocols] = _silu(acc)
            return carry

        lax.fori_loop(0, tb // CHUNK, tile, 0, unroll=4)


def _mods_kernel(c_ref, w_ref, b_ref, o_ref):
    o_ref[...] = _mm(_silu(c_ref[...]), w_ref[...]) + b_ref[...]


def _mods(cond, ada_w, ada_b):
    depth, d, n = ada_w.shape
    r = cond.shape[0]
    nt = n // d
    return pl.pallas_call(
        _mods_kernel,
        grid=(depth, nt),
        in_specs=[pl.BlockSpec((r, d), lambda l, j: (0, 0)),
                  pl.BlockSpec((None, d, d), lambda l, j: (l, 0, j)),
                  pl.BlockSpec((None, 1, d), lambda l, j: (l, 0, j))],
        out_specs=pl.BlockSpec((None, r, d), lambda l, j: (l, 0, j)),
        out_shape=jax.ShapeDtypeStruct((depth, r, n), F32),
        name="adaln_mods",
    )(cond, ada_w, ada_b.reshape(depth, 1, n))


def _token_tile(t):
    return TOKEN_TILE if t % TOKEN_TILE == 0 else TOKEN_BLOCK


def _token_grid_specs(bsz, t, stream):
    tm = _token_tile(t)
    tok = lambda width: pl.BlockSpec((None, tm, width), lambda b, i: (b, i, 0))
    mod = pl.BlockSpec((None, 6, D_MODEL), lambda b, i: (2 * b + stream, 0, 0))
    return (bsz, t // tm), tok, mod


def _const_spec(shape):
    zeros = (0,) * len(shape)
    return pl.BlockSpec(shape, lambda b, i: zeros, pipeline_mode=pl.Buffered(1))


_TOKEN_PARAMS = pltpu.CompilerParams(dimension_semantics=("parallel", "parallel"), vmem_limit_bytes=VMEM_LIMIT)


def _inproj_kernel(h_ref, mod_ref, nw_ref, w0, w1, w2, w3, o0, o1, o2, o3):
    xm = _norm_modulate(h_ref[...], nw_ref[...], mod_ref[0:1, :], mod_ref[1:2, :]).astype(BF16)
    for w, o in ((w0, o0), (w1, o1), (w2, o2), (w3, o3)):
        o[...] = jnp.dot(xm, w[...], preferred_element_type=F32)


def _inproj(h, mods, norm_w, ws, stream):
    bsz, t, d = h.shape
    grid, tok, mod = _token_grid_specs(bsz, t, stream)
    return pl.pallas_call(
        _inproj_kernel,
        grid=grid,
        in_specs=[tok(d), mod, _const_spec((1, d))] + [_const_spec(w.shape) for w in ws],
        out_specs=[tok(w.shape[1]) for w in ws],
        out_shape=[jax.ShapeDtypeStruct((bsz, t, w.shape[1]), F32) for w in ws],
        compiler_params=_TOKEN_PARAMS,
        name="in_proj",
    )(h, mods, norm_w.reshape(1, d), *ws)


def _scan_call(kernel, p, params, states, scratch_fn, colmajor, conv, name):
    bsz, t, c = p.shape
    if colmajor:
        rows = t // GRID_W
        assert rows == CHUNK and GRID_W % COLS_PER_BLOCK == 0
        geo = Geo(GRID_W // COLS_PER_BLOCK, COLS_PER_BLOCK * CHUNK, True, c)
        pv = p.reshape(bsz, rows, GRID_W, c)
        yshape = (bsz, rows, GRID_W, MIX_W)
        last_halo = rows // CONV_HALO - 1

        def specs(blk):
            main = pl.BlockSpec((None, rows, COLS_PER_BLOCK, c), lambda b, i: (b, 0, blk(i), 0))
            if not conv:
                return [main]
            col0, width = conv
            assert col0 % width == 0
            prev = pl.BlockSpec((None, CONV_HALO, COLS_PER_BLOCK, width),
                                lambda b, i: (b, last_halo, jnp.maximum(blk(i) - 1, 0), col0 // width))
            nxt = pl.BlockSpec((None, CONV_HALO, COLS_PER_BLOCK, width),
                               lambda b, i: (b, 0, jnp.minimum(blk(i) + 1, geo.nblk - 1), col0 // width))
            return [main, prev, nxt]

        yspec = lambda blk: pl.BlockSpec((None, rows, COLS_PER_BLOCK, MIX_W), lambda b, i: (b, 0, blk(i), 0))
    else:
        assert t % TOKEN_BLOCK == 0
        tbk = _token_tile(t)
        geo = Geo(t // tbk, tbk, False, c)
        pv = p
        yshape = (bsz, t, MIX_W)
        per_halo = tbk // CONV_HALO
        nhalo = t // CONV_HALO

        def specs(blk):
            main = pl.BlockSpec((None, tbk, c), lambda b, i: (b, blk(i), 0))
            prev = pl.BlockSpec((None, CONV_HALO, c), lambda b, i: (b, jnp.maximum(blk(i) * per_halo - 1, 0), 0))
            nxt = pl.BlockSpec((None, CONV_HALO, c),
                               lambda b, i: (b, jnp.minimum((blk(i) + 1) * per_halo, nhalo - 1), 0))
            return [main, prev, nxt] if conv else [main]

        yspec = lambda blk: pl.BlockSpec((None, tbk, MIX_W), lambda b, i: (b, blk(i), 0))

    fwd = lambda i: i
    rev = lambda i: geo.nblk - 1 - i
    n_p = 3 if conv else 1
    state_specs = [pl.BlockSpec((None,) + s.shape[1:], lambda b, i, n=s.ndim - 1: (b,) + (0,) * n) for s in states]
    yout = jax.ShapeDtypeStruct(yshape, F32)
    outs = pl.pallas_call(
        functools.partial(kernel, geo),
        grid=(bsz, geo.nblk),
        in_specs=specs(fwd) + specs(rev) + [_const_spec(w.shape) for w in params] + state_specs,
        out_specs=[yspec(fwd), yspec(rev)] + state_specs,
        out_shape=[yout, yout] + [jax.ShapeDtypeStruct(s.shape, F32) for s in states],
        scratch_shapes=scratch_fn(geo),
        compiler_params=pltpu.CompilerParams(dimension_semantics=("parallel", "arbitrary"),
                                             vmem_limit_bytes=VMEM_LIMIT),
        name=name,
    )(*([pv] * (2 * n_p)), *params, *states)
    return outs[0].reshape(bsz, t, MIX_W), outs[1].reshape(bsz, t, MIX_W), tuple(outs[2:])


def _y_targets(geo, yf, yb, ysc):
    return (ysc.at[0], ysc.at[1]) if geo.colmajor else (yf, yb)


def _y_finish(geo, yf, yb, ysc):
    if geo.colmajor:
        _unstage(yf, ysc.at[0], geo)
        _unstage(yb, ysc.at[1], geo)


def _ssd_kernel(geo, pf, pfp, pfn, pb, pbp, pbn, cw, cb, dtb, nega, dskip, s_in,
                yf, yb, s_ref, xpad, xbc, cum_s, v_s):
    i = pl.program_id(1)
    tb = geo.tb
    nch = tb // CHUNK
    hd = SSD_HEAD_DIM
    assert hd == CHUNK

    @pl.when(i == 0)
    def _():
        s_ref[...] = s_in[...]

    for d, (p, pp, pn) in enumerate(((pf, pfp, pfn), (pb, pbp, pbn))):
        j = i if d == 0 else geo.nblk - 1 - i
        _conv_silu(xbc.at[d], xpad, p, pp, pn, MIX_W, 2 * MIX_W, cw, cb, j != 0, j != geo.nblk - 1, geo)
        delta = _softplus(p[:, 3 * MIX_W:3 * MIX_W + LANES] + dtb[...])
        cum_s[d] = _chunk_cumsum(nega[...] * delta, d)
        for h in range(SSD_HEADS):
            lane = d * SSD_HEADS + h
            v_s[d, :, h * hd:(h + 1) * hd] = xbc[d, :, h * hd:(h + 1) * hd] * delta[:, lane:lane + 1]

    hpg = SSD_HEADS // SSD_GROUPS
    gw = hpg * hd
    groups = [(d, g) for d in (0, 1) for g in range(SSD_GROUPS)]
    blk = _iota2((CHUNK, gw), 1) >> CHUNK_SHIFT
    col = _iota2((CHUNK, gw), 1) & (CHUNK - 1)
    row = _iota2((CHUNK, gw), 0)

    def group_cols(x, base):
        out = x[:, base:base + 1]
        for j in range(1, hpg):
            out = jnp.where(blk[0:x.shape[0]] == j, x[:, base + j:base + j + 1], out)
        return out

    def chunk(c, carry):
        dirs = []
        for d in (0, 1):
            rows = _chunk_rows(c, d, nch)
            cum = cum_s[d, rows, :]
            tot = cum[CHUNK - 1:CHUNK, :] if d == 0 else cum[0:1, :]
            cum_t = _rows_of(_one_hot_rows(SSD_HEADS, d * SSD_HEADS), jnp.concatenate([cum] * hpg, axis=0))
            dirs.append(dict(rows=rows, cum=cum, tot=tot, cum_t=cum_t))

        st = []
        for d, g in groups:
            dd = dirs[d]
            rows = dd["rows"]
            base = d * SSD_HEADS + g * hpg
            incl = (col <= row) if d == 0 else (col >= row)
            ccol = group_cols(dd["cum"], base)
            crow = dd["cum_t"][g * hpg:g * hpg + 1, :]
            for j in range(1, hpg):
                crow = jnp.where(blk[0:1] == j, dd["cum_t"][g * hpg + j:g * hpg + j + 1, :], crow)
            tot = group_cols(dd["tot"], base)
            decay = jnp.exp(jnp.where(incl, ccol - crow, -jnp.inf))
            bg = xbc[d, rows, MIX_W + g * SSD_STATE:MIX_W + (g + 1) * SSD_STATE]
            cg = xbc[d, rows, MIX_W + (SSD_GROUPS + g) * SSD_STATE:MIX_W + (SSD_GROUPS + g + 1) * SSD_STATE]
            vg = v_s[d, rows, g * gw:(g + 1) * gw]
            scores = _mm_nt(cg, jnp.concatenate([bg] * hpg, axis=0))
            inter = _mm(cg, s_ref[d, g])
            v_bd = jnp.concatenate([jnp.where(blk == j, vg, 0.0) for j in range(hpg)], axis=0)
            st.append(dict(p=scores * decay, inter=inter * jnp.exp(ccol), v_bd=v_bd, bg=bg,
                           v_end=vg * jnp.exp(tot - ccol), e_tot=jnp.exp(tot)))

        for (d, g), s in zip(groups, st):
            rows = dirs[d]["rows"]
            y = yf if d == 0 else yb
            yg = _mm(s["p"], s["v_bd"]) + s["inter"]
            if d == 0:
                yg = yg + xbc[d, rows, g * gw:(g + 1) * gw] * dskip[:, g * gw:(g + 1) * gw]
            y[rows, g * gw:(g + 1) * gw] = yg
            s_ref[d, g] = s_ref[d, g] * s["e_tot"] + _mm_tn(s["bg"], s["v_end"])
        return carry

    lax.fori_loop(0, nch, chunk, 0, unroll=4)


def _ssd_scan(p, cw, cb, dtb, nega, dskip, state):
    def scratch(geo):
        return [pltpu.VMEM((geo.tb + 2 * CONV_HALO, LANES), F32),
                pltpu.VMEM((2, geo.tb, 2 * MIX_W), F32),
                pltpu.VMEM((2, geo.tb, LANES), F32),
                pltpu.VMEM((2, geo.tb, MIX_W), F32)]
    return _scan_call(_ssd_kernel, p, (cw, cb, dtb, nega, dskip), state, scratch, False, (MIX_W, 2 * MIX_W), "ssd_scan")


def _ssd_state(bsz):
    return (jnp.zeros((bsz, 2, SSD_GROUPS, SSD_STATE, MIX_W // SSD_GROUPS), F32),)


def _mlstm_kernel(geo, pf, pb, ib, fb, c_in, m_in, yf, yb, c_ref, m_ref, qkv, b_s, li_s, cm_s, ysc):
    i = pl.program_id(1)
    tb = geo.tb
    nch = tb // CHUNK
    nh, dk, dv = MLSTM_HEADS, MLSTM_K, MLSTM_V
    kscale = dk ** -0.5
    k0, v0, g0 = nh * dk, 2 * nh * dk, 2 * nh * dk + 2 * MIX_W
    fl = 2 * nh

    @pl.when(i == 0)
    def _():
        c_ref[...] = c_in[...]
        m_ref[...] = m_in[...]

    for d, p in enumerate((pf, pb)):
        _stage(qkv.at[d], 0, p, 0, g0 - MIX_W, geo)
        _stage(li_s.at[d], 0, p, g0, g0 + LANES, geo)
        gates = li_s[d]
        li = pltpu.roll(gates + ib[...], fl, axis=1)
        b = _chunk_cumsum(_log_sigmoid(gates + fb[...]), d)
        li_s[d] = li
        b_s[d] = b
        cm_s[d] = _chunk_scan(li - b, d, jnp.maximum, -jnp.inf)

    ones_col = jnp.where(_iota2((CHUNK, dv), 1) == 0, 1.0, 0.0).astype(F32)
    npair = nh // 2
    pairs = [(d, hp) for d in (0, 1) for hp in range(npair)]
    lane_hi = _iota2((CHUNK, 2 * CHUNK), 1) >= CHUNK
    col = _iota2((CHUNK, 2 * CHUNK), 1) & (CHUNK - 1)
    row = _iota2((CHUNK, 2 * CHUNK), 0)
    ew = 2 * dv
    ydst = _y_targets(geo, yf, yb, ysc)

    def chunk(c, carry):
        dirs = []
        for d in (0, 1):
            rows = _chunk_rows(c, d, nch)
            sel = _one_hot_rows(8, fl + d * nh)
            b = b_s[d, rows, :]
            li = li_s[d, rows, :]
            tot = b[CHUNK - 1:CHUNK, :] if d == 0 else b[0:1, :]
            m_row = m_ref[d:d + 1, :]
            end_log = tot - b + li
            m_new = jnp.maximum(tot + m_row, jnp.max(end_log, axis=0, keepdims=True))
            m_ref[d:d + 1, :] = m_new
            inter = b + m_row
            bli = b - li
            dirs.append(dict(rows=rows, b=b, inter=inter, m_t=jnp.maximum(inter, b + cm_s[d, rows, :]),
                             bli_t=_rows_of(sel, jnp.concatenate([bli, bli], axis=0)),
                             w_end=jnp.exp(end_log - m_new), keep=jnp.exp(tot + m_row - m_new)))

        st = []
        for d, hp in pairs:
            g = dirs[d]
            la = fl + d * nh + 2 * hp
            incl = (col <= row) if d == 0 else (col >= row)
            bli_row = jnp.where(lane_hi[0:1], g["bli_t"][2 * hp + 1:2 * hp + 2, :], g["bli_t"][2 * hp:2 * hp + 1, :])
            d_log = jnp.where(incl, _pair_cols(g["b"], lane_hi, la) - bli_row, -jnp.inf)
            m_pair = _pair_cols(g["m_t"], lane_hi, la)
            q = qkv[d, g["rows"], hp * 2 * dk:(hp + 1) * 2 * dk]
            k = qkv[d, g["rows"], k0 + hp * 2 * dk:k0 + (hp + 1) * 2 * dk] * kscale
            vext = [jnp.concatenate([qkv[d, g["rows"], v0 + (2 * hp + h) * dv:v0 + (2 * hp + h + 1) * dv], ones_col],
                                    axis=1) for h in (0, 1)]
            scores = _mm_nt(q, _block_diag2(k, lane_hi)) * jnp.exp(d_log - m_pair)
            qw = q * jnp.exp(_pair_cols(g["inter"], lane_hi, la) - m_pair)
            st.append(dict(lhs=jnp.concatenate([scores, qw], axis=1), k=k, vext=vext))

        res = [_mm(s["lhs"], jnp.concatenate([_block_diag_wide(s["vext"][0], s["vext"][1]), c_ref[d, hp]], axis=0))
               for (d, hp), s in zip(pairs, st)]

        for (d, hp), s, r in zip(pairs, st, res):
            g = dirs[d]
            la = fl + d * nh + 2 * hp
            kw = s["k"] * _pair_cols(g["w_end"], lane_hi, la)
            for h in (0, 1):
                den = jnp.maximum(jnp.abs(r[:, h * ew + dv:h * ew + dv + 1]), jnp.exp(-g["m_t"][:, la + h:la + h + 1]))
                ydst[d][g["rows"], (2 * hp + h) * dv:(2 * hp + h + 1) * dv] = r[:, h * ew:h * ew + dv] / den
                kw_h = jnp.where(lane_hi, kw, 0.0) if h == 1 else jnp.where(lane_hi, 0.0, kw)
                cs = slice(h * ew, (h + 1) * ew)
                c_ref[d, hp, :, cs] = c_ref[d, hp, :, cs] * g["keep"][:, la + h:la + h + 1] + _mm_tn(kw_h, s["vext"][h])
        return carry

    lax.fori_loop(0, nch, chunk, 0, unroll=4)
    _y_finish(geo, yf, yb, ysc)


def _mlstm_scan(p, ib, fb, state, colmajor):
    def scratch(geo):
        return [pltpu.VMEM((2, geo.tb, 2 * MLSTM_HEADS * MLSTM_K + MIX_W), F32),
                pltpu.VMEM((2, geo.tb, LANES), F32),
                pltpu.VMEM((2, geo.tb, LANES), F32),
                pltpu.VMEM((2, geo.tb, LANES), F32),
                pltpu.VMEM((2, geo.tb, MIX_W) if geo.colmajor else (2, 8, LANES), F32)]
    return _scan_call(_mlstm_kernel, p, (ib, fb), state, scratch, colmajor, None, "mlstm_scan")


def _mlstm_state(bsz):
    return (jnp.zeros((bsz, 2, MLSTM_HEADS // 2, 2 * MLSTM_K, 4 * MLSTM_V), F32), jnp.zeros((bsz, 8, LANES), F32))


GDN_PREP_CHUNKS = 4


def _gdn_kernel(geo, pf, pfp, pfn, pb, pbp, pbn, cw, dtb, nega, s_in, yf, yb,
                s_ref, xpad, qkv, cum_s, beta_s, lvl, sol_s, attn_s, ysc):
    i = pl.program_id(1)
    tb = geo.tb
    nch = tb // CHUNK
    nh, hd = GDN_HEADS, GDN_HEAD_DIM
    g0 = 4 * MIX_W
    npair = nh // 2
    pw = 2 * hd

    @pl.when(i == 0)
    def _():
        s_ref[...] = s_in[...]
        r, c = _iota2((CHUNK, 2 * CHUNK), 0), _iota2((CHUNK, 2 * CHUNK), 1) & (CHUNK - 1)
        for k in range(lvl.shape[0]):
            joined = ((r >> (k + 1)) == (c >> (k + 1))) & (((r >> k) & 1) != ((c >> k) & 1))
            lvl[k] = jnp.where(joined, 1.0, 0.0).astype(F32)

    for d, (p, pp, pn) in enumerate(((pf, pfp, pfn), (pb, pbp, pbn))):
        j = i if d == 0 else geo.nblk - 1 - i
        _conv_silu(qkv.at[d], xpad, p, pp, pn, 0, 3 * MIX_W, cw, None, j != 0, j != geo.nblk - 1, geo)
        for h in range(2 * nh):
            hs = slice(h * hd, (h + 1) * hd)
            x = qkv[d, :, hs]
            x = x * lax.rsqrt(jnp.sum(x * x, axis=-1, keepdims=True) + 1e-6)
            qkv[d, :, hs] = x * (hd ** -0.5) if h < nh else x
        _stage(beta_s.at[d], 0, p, g0, g0 + LANES, geo)
        gates = beta_s[d]
        cum_s[d] = _chunk_cumsum(nega[...] * _softplus(gates + dtb[...]), d)
        beta_s[d] = pltpu.roll(_sigmoid(gates), LANES - 2 * nh, axis=1)

    lane_hi = _iota2((CHUNK, 2 * CHUNK), 1) >= CHUNK
    col = _iota2((CHUNK, 2 * CHUNK), 1) & (CHUNK - 1)
    row = _iota2((CHUNK, 2 * CHUNK), 0)
    ydst = _y_targets(geo, yf, yb, ysc)

    def gate_cols(c, d):
        rows = _chunk_rows(c, d, nch)
        cum = cum_s[d, rows, :]
        tot = cum[CHUNK - 1:CHUNK, :] if d == 0 else cum[0:1, :]
        return rows, cum, tot

    def heads(d, rows, base, hp):
        return [qkv[d, rows, base + (2 * hp + h) * hd:base + (2 * hp + h + 1) * hd] for h in (0, 1)]

    ngroup = min(nch, GDN_PREP_CHUNKS)
    def prep(it, carry):
        dirs = {}
        for cc in range(ngroup):
            for d in (0, 1):
                rows, cum, tot = gate_cols(it * ngroup + cc, d)
                cum_t = _rows_of(_one_hot_rows(8, d * nh), jnp.concatenate([cum, cum], axis=0))
                dirs[cc, d] = dict(rows=rows, cum=cum, beta=beta_s[d, rows, :], cum_t=cum_t, e_cum=jnp.exp(cum))
        units = [(cc, d, hp) for cc in range(ngroup) for d in (0, 1) for hp in range(npair)]

        st = []
        for cc, d, hp in units:
            g = dirs[cc, d]
            la = d * nh + 2 * hp
            incl = (col <= row) if d == 0 else (col >= row)
            strict = (col < row) if d == 0 else (col > row)
            crow = jnp.where(lane_hi[0:1], g["cum_t"][2 * hp + 1:2 * hp + 2, :], g["cum_t"][2 * hp:2 * hp + 1, :])
            decay = jnp.exp(jnp.where(incl, _pair_cols(g["cum"], lane_hi, la) - crow, -jnp.inf))
            q, k, v = (heads(d, g["rows"], base, hp) for base in (0, MIX_W, 2 * MIX_W))
            bcol = [g["beta"][:, la + h:la + h + 1] for h in (0, 1)]
            ecol = [g["e_cum"][:, la + h:la + h + 1] for h in (0, 1)]
            kb = [k[h] * bcol[h] for h in (0, 1)]
            lhs = jnp.concatenate([jnp.concatenate(kb, axis=1), jnp.concatenate(q, axis=1)], axis=0)
            prod = _mm_nt(lhs, _block_diag_wide(k[0], k[1]))
            attn_s[d, hp, g["rows"], :] = prod[CHUNK:2 * CHUNK] * decay
            rhs = [jnp.concatenate([kb[h] * ecol[h], v[h] * bcol[h]], axis=1) for h in (0, 1)]
            st.append(dict(a=jnp.where(strict, prod[0:CHUNK] * decay, 0.0), rhs=rhs))

        es = [-(s["a"] * lvl[0]) for s in st]
        for lv in range(1, lvl.shape[0]):
            aks = [s["a"] * lvl[lv] for s in st]
            xs = [ak + _mm(ak, _block_diag2(e, lane_hi)) for ak, e in zip(aks, es)]
            es = [e - (x + _mm(e, _block_diag2(x, lane_hi))) for e, x in zip(es, xs)]

        for (cc, d, hp), s, e in zip(units, st, es):
            rhs = s["rhs"]
            sol_s[d, hp, dirs[cc, d]["rows"], :] = (jnp.concatenate(rhs, axis=1)
                                                   + _mm(e, _block_diag_wide(rhs[0], rhs[1])))
        return carry

    lax.fori_loop(0, nch // ngroup, prep, 0)

    pairs = [(d, hp) for d in (0, 1) for hp in range(npair)]

    def chunk(c, carry):
        dirs = []
        for d in (0, 1):
            rows, cum, tot = gate_cols(c, d)
            dirs.append(dict(rows=rows, e_cum=jnp.exp(cum), e_end=jnp.exp(tot - cum), e_tot=jnp.exp(tot)))
        rs, sols, ks = [], [], []
        for d, hp in pairs:
            g = dirs[d]
            la = d * nh + 2 * hp
            sol = sol_s[d, hp, g["rows"], :]
            q = heads(d, g["rows"], 0, hp)
            qe = jnp.concatenate([q[h] * g["e_cum"][:, la + h:la + h + 1] for h in (0, 1)], axis=1)
            w = jnp.concatenate([sol[:, 0:hd], sol[:, 2 * hd:3 * hd]], axis=1)
            rs.append(_mm(jnp.concatenate([w, qe], axis=0), s_ref[d, hp]))
            sols.append(sol)
        for (d, hp), sol, r in zip(pairs, sols, rs):
            g = dirs[d]
            la = d * nh + 2 * hp
            k = heads(d, g["rows"], MIX_W, hp)
            v_new = [sol[:, hd:2 * hd] - r[0:CHUNK, 0:hd], sol[:, 3 * hd:4 * hd] - r[0:CHUNK, hd:pw]]
            ydst[d][g["rows"], hp * pw:(hp + 1) * pw] = (
                r[CHUNK:2 * CHUNK] + _mm(attn_s[d, hp, g["rows"], :], _block_diag_wide(v_new[0], v_new[1])))
            for h in (0, 1):
                blk = slice(h * hd, (h + 1) * hd)
                k_end = k[h] * g["e_end"][:, la + h:la + h + 1]
                s_ref[d, hp, blk, blk] = (s_ref[d, hp, blk, blk] * g["e_tot"][:, la + h:la + h + 1]
                                          + _mm_tn(k_end, v_new[h]))
        return carry

    lax.fori_loop(0, nch, chunk, 0, unroll=4)
    _y_finish(geo, yf, yb, ysc)


def _gdn_scan(p, cw, dtb, nega, state, colmajor):
    nlevels = CHUNK.bit_length() - 1
    pw = 2 * GDN_HEAD_DIM

    def scratch(geo):
        return [pltpu.VMEM((geo.tb + 2 * CONV_HALO, LANES), F32),
                pltpu.VMEM((2, geo.tb, 3 * MIX_W), F32),
                pltpu.VMEM((2, geo.tb, LANES), F32),
                pltpu.VMEM((2, geo.tb, LANES), F32),
                pltpu.VMEM((nlevels, CHUNK, 2 * CHUNK), F32),
                pltpu.VMEM((2, GDN_HEADS // 2, geo.tb, 2 * pw), F32),
                pltpu.VMEM((2, GDN_HEADS // 2, geo.tb, 2 * CHUNK), F32),
                pltpu.VMEM((2, geo.tb, MIX_W) if geo.colmajor else (2, 8, LANES), F32)]
    return _scan_call(_gdn_kernel, p, (cw, dtb, nega), state, scratch, colmajor, (0, 3 * MIX_W), "gdn_scan")


def _gdn_state(bsz):
    return (jnp.zeros((bsz, 2, GDN_HEADS // 2, 2 * GDN_HEAD_DIM, 2 * GDN_HEAD_DIM), F32),)


def _gla_kernel(geo, pf, pb, up, gb, st_in, yf, yb, st_ref, cum_s):
    i = pl.program_id(1)
    tb = geo.tb
    nch = tb // CHUNK
    nh, dk, dv = GLA_HEADS, GLA_K, GLA_V
    qscale = dk ** -0.5
    q0, k0, v0, g0 = 0, nh * dk, 2 * nh * dk, 2 * nh * dk + 2 * MIX_W
    mid = CHUNK // 2

    @pl.when(i == 0)
    def _():
        st_ref[...] = st_in[...]

    for d, p in enumerate((pf, pb)):
        gk = _mm(p[:, g0:g0 + LANES], up[d]) + gb[d:d + 1, :]
        cum_s[d] = _chunk_cumsum(_log_sigmoid(gk) * (1.0 / GLA_GATE_NORM), d)

    npair = nh // 2
    pairs = [(d, hp) for d in (0, 1) for hp in range(npair)]
    lane_hi = _iota2((CHUNK, 2 * CHUNK), 1) >= CHUNK
    col = _iota2((CHUNK, 2 * CHUNK), 1) & (CHUNK - 1)
    row = _iota2((CHUNK, 2 * CHUNK), 0)
    assert dk == CHUNK

    def chunk(c, carry):
        st = []
        for d, hp in pairs:
            p = pf if d == 0 else pb
            rows = _chunk_rows(c, d, nch)
            ps = slice(hp * 2 * dk, (hp + 1) * 2 * dk)
            cum = cum_s[d, rows, ps]
            ref = cum[mid:mid + 1, :] if d == 0 else cum[CHUNK - 1 - mid:CHUNK - mid, :]
            tot = cum[CHUNK - 1:CHUNK, :] if d == 0 else cum[0:1, :]
            q = p[rows, q0 + hp * 2 * dk:q0 + (hp + 1) * 2 * dk] * qscale
            k = p[rows, k0 + hp * 2 * dk:k0 + (hp + 1) * 2 * dk]
            v = [p[rows, v0 + (2 * hp + h) * dv:v0 + (2 * hp + h + 1) * dv] for h in (0, 1)]
            incl = (col <= row) if d == 0 else (col >= row)
            kg_bd = _block_diag2(k * jnp.exp(ref - cum), lane_hi)
            scores = jnp.where(incl, _mm_nt(q * jnp.exp(cum - ref), kg_bd), 0.0)
            st.append(dict(rows=rows, scores=scores, v=v, qe=q * jnp.exp(cum), ke=k * jnp.exp(tot - cum),
                           e_tot=jnp.exp(tot)))

        for (d, hp), s in zip(pairs, st):
            y = yf if d == 0 else yb
            y[s["rows"], hp * 2 * dv:(hp + 1) * 2 * dv] = (_mm(s["scores"], _block_diag_wide(s["v"][0], s["v"][1]))
                                                          + _mm_nt(s["qe"], st_ref[d, hp]))
            for h in (0, 1):
                ke_h = jnp.where(lane_hi, s["ke"], 0.0) if h == 1 else jnp.where(lane_hi, 0.0, s["ke"])
                rs = slice(h * dv, (h + 1) * dv)
                st_ref[d, hp, rs, :] = st_ref[d, hp, rs, :] * s["e_tot"] + _mm_tn(s["v"][h], ke_h)
        return carry

    lax.fori_loop(0, nch, chunk, 0, unroll=4)


def _gla_scan(p, up, gb, state):
    def scratch(geo):
        return [pltpu.VMEM((2, geo.tb, GLA_HEADS * GLA_K), F32)]
    return _scan_call(_gla_kernel, p, (up, gb), state, scratch, False, None, "gla_scan")


def _gla_state(bsz):
    return (jnp.zeros((bsz, 2, GLA_HEADS // 2, 2 * GLA_V, 2 * GLA_K), F32),)


def _merge_kernel(h_ref, mod_ref, nw_ref,
                  ssd_f, ssd_b, ml_f, ml_b, gd_f, gd_b, gl_f, gl_b,
                  z_ref, o_ref, gg_ref, lg_ref,
                  nssd, nml, ngd, ngl, wg_ref, wb_ref, wo_ref, out_ref):
    x = h_ref[...]
    xm = _norm_modulate(x, nw_ref[...], mod_ref[0:1, :], mod_ref[1:2, :]).astype(BF16)
    branches = (
        _group_rmsnorm((ssd_f[...] + ssd_b[...]) * _silu(z_ref[...]), nssd[...], SSD_GROUPS),
        _group_rmsnorm(ml_f[...] + ml_b[...], nml[...], MLSTM_HEADS) * _sigmoid(o_ref[...]),
        _group_rmsnorm(gd_f[...] + gd_b[...], ngd[...], GDN_HEADS) * _silu(gg_ref[...]),
        _group_rmsnorm(gl_f[...] + gl_b[...], ngl[...], GLA_HEADS) * _silu(lg_ref[...]),
    )
    u = None
    for n, br in enumerate(branches):
        gate = _sigmoid(jnp.dot(xm, wg_ref[:, n * D_MODEL:(n + 1) * D_MODEL], preferred_element_type=F32))
        term = gate * _mm(br, wb_ref[n])
        u = term if u is None else u + term
    out_ref[...] = x + mod_ref[2:3, :] * _mm(u, wo_ref[...])


def _merge(h, mods, norm_w, ys, p_ssd, p_ml, p_gd, p_gl, norms, wg, wb, wo, stream):
    bsz, t, d = h.shape
    grid, tok, mod = _token_grid_specs(bsz, t, stream)
    gate_spec = lambda blk: pl.BlockSpec((None, _token_tile(t), MIX_W), lambda b, i: (b, i, blk))
    return pl.pallas_call(
        _merge_kernel,
        grid=grid,
        in_specs=([tok(d), mod, _const_spec((1, d))] + [tok(MIX_W)] * 8
                  + [gate_spec(0), gate_spec(2), gate_spec(3), gate_spec(2)]
                  + [_const_spec((1, MIX_W))] * 4
                  + [_const_spec(wg.shape), _const_spec(wb.shape), _const_spec(wo.shape)]),
        out_specs=tok(d),
        out_shape=jax.ShapeDtypeStruct((bsz, t, d), F32),
        compiler_params=_TOKEN_PARAMS,
        name="merge_out",
    )(h, mods, norm_w.reshape(1, d), *ys, p_ssd, p_ml, p_gd, p_gl,
      *[n.reshape(1, MIX_W) for n in norms], wg, wb, wo)


def _ffn_body(h_ref, mod_ref, nw_ref, wg_ref, wu_ref, wd_ref):
    x = h_ref[...]
    xm = _norm_modulate(x, nw_ref[...], mod_ref[3:4, :], mod_ref[4:5, :]).astype(BF16)
    a = jnp.dot(xm, wg_ref[...], preferred_element_type=F32)
    b = jnp.dot(xm, wu_ref[...], preferred_element_type=F32)
    return x + mod_ref[5:6, :] * _mm(_silu(a) * b, wd_ref[...])


def _ffn_kernel(h_ref, mod_ref, nw_ref, wg_ref, wu_ref, wd_ref, out_ref):
    out_ref[...] = _ffn_body(h_ref, mod_ref, nw_ref, wg_ref, wu_ref, wd_ref)


def _ffn_final_kernel(h_ref, mod_ref, nw_ref, wg_ref, wu_ref, wd_ref, nf_ref, out_ref):
    y = _ffn_body(h_ref, mod_ref, nw_ref, wg_ref, wu_ref, wd_ref)
    out_ref[...] = y * lax.rsqrt(jnp.mean(y * y, axis=-1, keepdims=True) + 1e-6) * nf_ref[...]


def _ffn(h, mods, norm_w, wg, wu, wd, stream, final_norm_w=None):
    bsz, t, d = h.shape
    grid, tok, mod = _token_grid_specs(bsz, t, stream)
    extra = [] if final_norm_w is None else [final_norm_w.reshape(1, d)]
    return pl.pallas_call(
        _ffn_kernel if final_norm_w is None else _ffn_final_kernel,
        grid=grid,
        in_specs=[tok(d), mod, _const_spec((1, d)), _const_spec(wg.shape), _const_spec(wu.shape),
                  _const_spec(wd.shape)] + [_const_spec((1, d))] * len(extra),
        out_specs=tok(d),
        out_shape=jax.ShapeDtypeStruct((bsz, t, d), F32),
        compiler_params=_TOKEN_PARAMS,
        name="swiglu",
    )(h, mods, norm_w.reshape(1, d), wg, wu, wd, *extra)


def _pad_cols(w, total):
    return jnp.pad(w, ((0, 0), (0, total - w.shape[1])))


def _lane_row(values, lane0):
    v = values.reshape(-1).astype(F32)
    return jnp.zeros((1, LANES), F32).at[0, lane0:lane0 + v.shape[0]].set(v)


def kernel(x, c, ctx, c_ctx, ada_w, ada_b, norm_mix_w, w_in, ssd_conv_w, ssd_conv_b, ssd_dt_bias,
           ssd_a_log, ssd_d, ssd_norm_w, mlstm_i_bias, mlstm_f_bias, mlstm_norm_w, gdn_conv_w, gdn_a_log,
           gdn_dt_bias, gdn_norm_w, gla_gk_up, gla_gk_bias, gla_norm_w, w_branch, w_out, norm_ffn_w,
           ffn_w_gate, ffn_w_up, ffn_w_down, norm_f_w):
    bsz, seq, d = x.shape
    depth = w_in.shape[0]
    assert d == D_MODEL and ctx.shape[1] % TOKEN_BLOCK == 0 and seq % TOKEN_BLOCK == 0

    cond = jnp.stack([jnp.broadcast_to(c_ctx, c.shape), c], axis=1).reshape(2 * bsz, d)
    mods_all = _mods(cond, ada_w, ada_b).reshape(depth, 2 * bsz, 6, d)

    ssd_cols = 2 * MIX_W + 2 * SSD_GROUPS * SSD_STATE + 2 * SSD_HEADS
    ml_cols = 2 * MLSTM_HEADS * MLSTM_K + 2 * MIX_W + 4 * MLSTM_HEADS
    gd_cols = 4 * MIX_W + 4 * GDN_HEADS
    gl_cols = 2 * GLA_HEADS * GLA_K + 2 * MIX_W + 2 * GLA_RANK
    o1, o2, o3, o4 = ssd_cols, ssd_cols + ml_cols, ssd_cols + ml_cols + gd_cols, ssd_cols + ml_cols + gd_cols + gl_cols

    hs = [ctx, x]
    for l in range(depth):
        last = l == depth - 1
        wl = w_in[l]
        ws = (_pad_cols(wl[:, :o1], SSD_P).astype(BF16), _pad_cols(wl[:, o1:o2], MLSTM_P).astype(BF16),
              _pad_cols(wl[:, o2:o3], GDN_P).astype(BF16), _pad_cols(wl[:, o3:o4], GLA_P).astype(BF16))
        mods = mods_all[l]
        ssd_par = (ssd_conv_w[l], ssd_conv_b[l].reshape(1, -1), _lane_row(ssd_dt_bias[l], 0),
                   _lane_row(-jnp.exp(ssd_a_log[l]), 0), jnp.repeat(ssd_d[l], SSD_HEAD_DIM).reshape(1, MIX_W))
        ml_par = (_lane_row(mlstm_i_bias[l], 0), _lane_row(mlstm_f_bias[l], 2 * MLSTM_HEADS))
        gd_par = (gdn_conv_w[l], _lane_row(gdn_dt_bias[l], 0), _lane_row(-jnp.exp(gdn_a_log[l]), 0))
        up = jnp.zeros((2, LANES, GLA_HEADS * GLA_K), F32)
        for dd in range(2):
            up = up.at[dd, dd * GLA_RANK:(dd + 1) * GLA_RANK].set(gla_gk_up[l, dd])
        gl_par = (up.astype(BF16), gla_gk_bias[l])
        merge_w = (wl[:, o4:].astype(BF16), w_branch[l].astype(BF16), w_out[l].astype(BF16))
        ffn_w = (ffn_w_gate[l].astype(BF16), ffn_w_up[l].astype(BF16), ffn_w_down[l].astype(BF16))
        norms = (ssd_norm_w[l], mlstm_norm_w[l], gdn_norm_w[l], gla_norm_w[l])

        states = [_ssd_state(bsz), _mlstm_state(bsz), _gdn_state(bsz), _gla_state(bsz)]
        for stream in (0, 1):
            h = hs[stream]
            ps = _inproj(h, mods, norm_mix_w[l], ws, stream)
            cm = stream == 1
            ssd_y = _ssd_scan(ps[0], *ssd_par, states[0])
            ml_y = _mlstm_scan(ps[1], *ml_par, states[1], cm)
            gd_y = _gdn_scan(ps[2], *gd_par, states[2], cm)
            gl_y = _gla_scan(ps[3], *gl_par, states[3])
            states = [ssd_y[2], ml_y[2], gd_y[2], gl_y[2]]
            if stream == 0 and last:
                continue
            ys = ssd_y[:2] + ml_y[:2] + gd_y[:2] + gl_y[:2]
            h = _merge(h, mods, norm_mix_w[l], ys, *ps, norms, *merge_w, stream)
            hs[stream] = _ffn(h, mods, norm_ffn_w[l], *ffn_w, stream, norm_f_w if last else None)
    return hs[1]
```

```python
import functools
from typing import NamedTuple

import jax
import jax.numpy as jnp
from jax import lax
from jax.experimental import pallas as pl
from jax.experimental.pallas import tpu as pltpu

F32 = jnp.float32
BF16 = jnp.bfloat16
HIGHEST = lax.Precision.HIGHEST

D_MODEL = 1024
GRID_W = 64
CHUNK = 64
CHUNK_SHIFT = CHUNK.bit_length() - 1
CONV_K = 5
CONV_HALO = 8
MIX_W = 512
LANES = 128
TOKEN_BLOCK = 256
TOKEN_TILE = 512
COLS_PER_BLOCK = 8
FFN_HIDDEN = 2816
VMEM_LIMIT = 56 * 1024 * 1024

SSD_HEADS, SSD_HEAD_DIM, SSD_STATE, SSD_GROUPS = 8, 64, 128, 2
MLSTM_HEADS, MLSTM_K, MLSTM_V = 4, 64, 128
GDN_HEADS, GDN_HEAD_DIM = 4, 128
GLA_HEADS, GLA_K, GLA_V, GLA_RANK, GLA_GATE_NORM = 4, 64, 128, 16, 16

SSD_P = 2 * MIX_W + 2 * SSD_GROUPS * SSD_STATE + LANES
MLSTM_P = 2 * MLSTM_HEADS * MLSTM_K + 2 * MIX_W + LANES
GDN_P = 4 * MIX_W + LANES
GLA_P = 2 * GLA_HEADS * GLA_K + 2 * MIX_W + LANES


class Geo(NamedTuple):
    nblk: int
    tb: int
    colmajor: bool
    c: int


def _mm(a, b):
    return jnp.dot(a.astype(BF16), b.astype(BF16), preferred_element_type=F32)


def _mm_nt(a, b):
    return lax.dot_general(a.astype(BF16), b.astype(BF16), (((1,), (1,)), ((), ())),
                           preferred_element_type=F32)


def _mm_tn(a, b):
    return lax.dot_general(a.astype(BF16), b.astype(BF16), (((0,), (0,)), ((), ())),
                           preferred_element_type=F32)


def _chunk_scan(x, d, op, identity):
    n = x.shape[0]
    pos = _iota2(x.shape, 0) & (CHUNK - 1)
    sh = 1
    while sh < CHUNK:
        if d == 0:
            x = op(x, jnp.where(pos >= sh, pltpu.roll(x, sh, axis=0), identity))
        else:
            x = op(x, jnp.where(pos < CHUNK - sh, pltpu.roll(x, n - sh, axis=0), identity))
        sh *= 2
    return x


def _chunk_cumsum(x, d):
    return _chunk_scan(x, d, jnp.add, 0.0)


def _rows_of(sel, x):
    return lax.dot_general(sel, x, (((1,), (1,)), ((), ())), precision=HIGHEST,
                           preferred_element_type=F32)


def _sigmoid(x):
    return 1.0 / (1.0 + jnp.exp(-x))


def _silu(x):
    return x * _sigmoid(x)


def _softplus(x):
    return jnp.maximum(x, 0.0) + jnp.log1p(jnp.exp(-jnp.abs(x)))


def _log_sigmoid(x):
    return jnp.minimum(x, 0.0) - jnp.log1p(jnp.exp(-jnp.abs(x)))


def _iota2(shape, axis):
    return lax.broadcasted_iota(jnp.int32, shape, axis)


def _one_hot_rows(nrows, lane0):
    r, c = _iota2((nrows, LANES), 0), _iota2((nrows, LANES), 1)
    return jnp.where(c == r + lane0, 1.0, 0.0).astype(F32)


def _chunk_rows(c, d, nch):
    r0 = c * CHUNK if d == 0 else (nch - 1 - c) * CHUNK
    return pl.ds(r0 if isinstance(r0, int) else pl.multiple_of(r0, CHUNK), CHUNK)


def _pair_cols(x, lane_hi, la):
    return jnp.where(lane_hi, x[:, la + 1:la + 2], x[:, la:la + 1])


def _block_diag2(x, lane_hi):
    return jnp.concatenate([jnp.where(lane_hi, 0.0, x), jnp.where(lane_hi, x, 0.0)], axis=0)


def _block_diag_wide(a, b):
    z = jnp.zeros_like(a)
    return jnp.concatenate([jnp.concatenate([a, z], axis=1), jnp.concatenate([z, b], axis=1)], axis=0)


def _group_rmsnorm(x, w, groups, eps=1e-6):
    gw = x.shape[-1] // groups
    outs = []
    for g in range(groups):
        xg = x[:, g * gw:(g + 1) * gw]
        ms = jnp.mean(xg * xg, axis=-1, keepdims=True)
        outs.append(xg * lax.rsqrt(ms + eps))
    return jnp.concatenate(outs, axis=-1) * w


def _norm_modulate(x, nw, shift, scale, eps=1e-6):
    ms = jnp.mean(x * x, axis=-1, keepdims=True)
    y = x * lax.rsqrt(ms + eps) * nw
    return y * (1.0 + scale) + shift


def _stage(dst_ref, row0, p_ref, c0, c1, geo):
    if not geo.colmajor:
        dst_ref[row0:row0 + geo.tb, :] = p_ref[:, c0:c1]
    else:
        for j in range(geo.tb // CHUNK):
            dst_ref[row0 + j * CHUNK:row0 + (j + 1) * CHUNK, :] = p_ref[:, j, c0:c1]


def _unstage(y_ref, ysc_ref, geo):
    for j in range(geo.tb // CHUNK):
        y_ref[:, j, :] = ysc_ref[j * CHUNK:(j + 1) * CHUNK, :]


def _conv_silu(dst_ref, xpad_ref, p_ref, prev_ref, next_ref, col0, width, w_ref, b_ref,
               left_valid, right_valid, geo):
    tb = geo.tb
    base = CONV_HALO - CONV_K // 2
    win = CHUNK + 2 * CONV_HALO
    for s in range(width // LANES):
        c0 = col0 + s * LANES
        ocols = slice(s * LANES, (s + 1) * LANES)
        if geo.colmajor:
            prev = prev_ref[:, COLS_PER_BLOCK - 1, c0 - col0:c0 - col0 + LANES]
            nxt = next_ref[:, 0, c0 - col0:c0 - col0 + LANES]
        else:
            prev, nxt = prev_ref[:, c0:c0 + LANES], next_ref[:, c0:c0 + LANES]
        xpad_ref[0:CONV_HALO, :] = jnp.where(left_valid, prev, 0.0)
        _stage(xpad_ref, CONV_HALO, p_ref, c0, c0 + LANES, geo)
        xpad_ref[CONV_HALO + tb:2 * CONV_HALO + tb, :] = jnp.where(right_valid, nxt, 0.0)
        taps = [w_ref[k:k + 1, ocols] for k in range(CONV_K)]
        bias = None if b_ref is None else b_ref[:, ocols]

        def tile(rt, carry):
            r0 = pl.multiple_of(rt * CHUNK, CHUNK)
            acc = taps[0] * xpad_ref[pl.ds(r0 + base, CHUNK), :]
            for k in range(1, CONV_K):
                acc = acc + taps[k] * xpad_ref[pl.ds(r0 + base + k, CHUNK), :]
            if bias is not None:
                acc = acc + bias
            dst_ref[pl.ds(r0, CHUNK), ocols] = _silu(acc)
            return carry

        lax.fori_loop(0, tb // CHUNK, tile, 0, unroll=4)


def _mods_kernel(c_ref, w_ref, b_ref, o_ref):
    o_ref[...] = _mm(_silu(c_ref[...]), w_ref[...]) + b_ref[...]


def _mods(cond, ada_w, ada_b):
    depth, d, n = ada_w.shape
    r = cond.shape[0]
    nt = n // d
    return pl.pallas_call(
        _mods_kernel,
        grid=(depth, nt),
        in_specs=[pl.BlockSpec((r, d), lambda l, j: (0, 0)),
                  pl.BlockSpec((None, d, d), lambda l, j: (l, 0, j)),
                  pl.BlockSpec((None, 1, d), lambda l, j: (l, 0, j))],
        out_specs=pl.BlockSpec((None, r, d), lambda l, j: (l, 0, j)),
        out_shape=jax.ShapeDtypeStruct((depth, r, n), F32),
        name="adaln_mods",
    )(cond, ada_w, ada_b.reshape(depth, 1, n))


def _token_tile(t):
    return TOKEN_TILE if t % TOKEN_TILE == 0 else TOKEN_BLOCK


def _token_grid_specs(bsz, t, stream):
    tm = _token_tile(t)
    tok = lambda width: pl.BlockSpec((None, tm, width), lambda b, i: (b, i, 0))
    mod = pl.BlockSpec((None, 6, D_MODEL), lambda b, i: (2 * b + stream, 0, 0))
    return (bsz, t // tm), tok, mod


def _const_spec(shape):
    zeros = (0,) * len(shape)
    return pl.BlockSpec(shape, lambda b, i: zeros, pipeline_mode=pl.Buffered(1))


_TOKEN_PARAMS = pltpu.CompilerParams(dimension_semantics=("parallel", "parallel"), vmem_limit_bytes=VMEM_LIMIT)


def _inproj_kernel(h_ref, mod_ref, nw_ref, w0, w1, w2, w3, o0, o1, o2, o3):
    xm = _norm_modulate(h_ref[...], nw_ref[...], mod_ref[0:1, :], mod_ref[1:2, :]).astype(BF16)
    for w, o in ((w0, o0), (w1, o1), (w2, o2), (w3, o3)):
        o[...] = jnp.dot(xm, w[...], preferred_element_type=F32)


def _inproj(h, mods, norm_w, ws, stream):
    bsz, t, d = h.shape
    grid, tok, mod = _token_grid_specs(bsz, t, stream)
    return pl.pallas_call(
        _inproj_kernel,
        grid=grid,
        in_specs=[tok(d), mod, _const_spec((1, d))] + [_const_spec(w.shape) for w in ws],
        out_specs=[tok(w.shape[1]) for w in ws],
        out_shape=[jax.ShapeDtypeStruct((bsz, t, w.shape[1]), F32) for w in ws],
        compiler_params=_TOKEN_PARAMS,
        name="in_proj",
    )(h, mods, norm_w.reshape(1, d), *ws)


def _scan_call(kernel, p, params, states, scratch_fn, colmajor, conv, name):
    bsz, t, c = p.shape
    if colmajor:
        rows = t // GRID_W
        assert rows == CHUNK and GRID_W % COLS_PER_BLOCK == 0
        geo = Geo(GRID_W // COLS_PER_BLOCK, COLS_PER_BLOCK * CHUNK, True, c)
        pv = p.reshape(bsz, rows, GRID_W, c)
        yshape = (bsz, rows, GRID_W, MIX_W)
        last_halo = rows // CONV_HALO - 1

        def specs(blk):
            main = pl.BlockSpec((None, rows, COLS_PER_BLOCK, c), lambda b, i: (b, 0, blk(i), 0))
            if not conv:
                return [main]
            col0, width = conv
            assert col0 % width == 0
            prev = pl.BlockSpec((None, CONV_HALO, COLS_PER_BLOCK, width),
                                lambda b, i: (b, last_halo, jnp.maximum(blk(i) - 1, 0), col0 // width))
            nxt = pl.BlockSpec((None, CONV_HALO, COLS_PER_BLOCK, width),
                               lambda b, i: (b, 0, jnp.minimum(blk(i) + 1, geo.nblk - 1), col0 // width))
            return [main, prev, nxt]

        yspec = lambda blk: pl.BlockSpec((None, rows, COLS_PER_BLOCK, MIX_W), lambda b, i: (b, 0, blk(i), 0))
    else:
        assert t % TOKEN_BLOCK == 0
        tbk = _token_tile(t)
        geo = Geo(t // tbk, tbk, False, c)
        pv = p
        yshape = (bsz, t, MIX_W)
        per_halo = tbk // CONV_HALO
        nhalo = t // CONV_HALO

        def specs(blk):
            main = pl.BlockSpec((None, tbk, c), lambda b, i: (b, blk(i), 0))
            prev = pl.BlockSpec((None, CONV_HALO, c), lambda b, i: (b, jnp.maximum(blk(i) * per_halo - 1, 0), 0))
            nxt = pl.BlockSpec((None, CONV_HALO, c),
                               lambda b, i: (b, jnp.minimum((blk(i) + 1) * per_halo, nhalo - 1), 0))
            return [main, prev, nxt] if conv else [main]

        yspec = lambda blk: pl.BlockSpec((None, tbk, MIX_W), lambda b, i: (b, blk(i), 0))

    fwd = lambda i: i
    rev = lambda i: geo.nblk - 1 - i
    n_p = 3 if conv else 1
    state_specs = [pl.BlockSpec((None,) + s.shape[1:], lambda b, i, n=s.ndim - 1: (b,) + (0,) * n) for s in states]
    yout = jax.ShapeDtypeStruct(yshape, F32)
    outs = pl.pallas_call(
        functools.partial(kernel, geo),
        grid=(bsz, geo.nblk),
        in_specs=specs(fwd) + specs(rev) + [_const_spec(w.shape) for w in params] + state_specs,
        out_specs=[yspec(fwd), yspec(rev)] + state_specs,
        out_shape=[yout, yout] + [jax.ShapeDtypeStruct(s.shape, F32) for s in states],
        scratch_shapes=scratch_fn(geo),
        compiler_params=pltpu.CompilerParams(dimension_semantics=("parallel", "arbitrary"),
                                             vmem_limit_bytes=VMEM_LIMIT),
        name=name,
    )(*([pv] * (2 * n_p)), *params, *states)
    return outs[0].reshape(bsz, t, MIX_W), outs[1].reshape(bsz, t, MIX_W), tuple(outs[2:])


def _y_targets(geo, yf, yb, ysc):
    return (ysc.at[0], ysc.at[1]) if geo.colmajor else (yf, yb)


def _y_finish(geo, yf, yb, ysc):
    if geo.colmajor:
        _unstage(yf, ysc.at[0], geo)
        _unstage(yb, ysc.at[1], geo)


def _ssd_kernel(geo, pf, pfp, pfn, pb, pbp, pbn, cw, cb, dtb, nega, dskip, s_in,
                yf, yb, s_ref, xpad, xbc, cum_s, v_s):
    i = pl.program_id(1)
    tb = geo.tb
    nch = tb // CHUNK
    hd = SSD_HEAD_DIM
    assert hd == CHUNK

    @pl.when(i == 0)
    def _():
        s_ref[...] = s_in[...]

    for d, (p, pp, pn) in enumerate(((pf, pfp, pfn), (pb, pbp, pbn))):
        j = i if d == 0 else geo.nblk - 1 - i
        _conv_silu(xbc.at[d], xpad, p, pp, pn, MIX_W, 2 * MIX_W, cw, cb, j != 0, j != geo.nblk - 1, geo)
        delta = _softplus(p[:, 3 * MIX_W:3 * MIX_W + LANES] + dtb[...])
        cum_s[d] = _chunk_cumsum(nega[...] * delta, d)
        for h in range(SSD_HEADS):
            lane = d * SSD_HEADS + h
            v_s[d, :, h * hd:(h + 1) * hd] = xbc[d, :, h * hd:(h + 1) * hd] * delta[:, lane:lane + 1]

    hpg = SSD_HEADS // SSD_GROUPS
    gw = hpg * hd
    groups = [(d, g) for d in (0, 1) for g in range(SSD_GROUPS)]
    blk = _iota2((CHUNK, gw), 1) >> CHUNK_SHIFT
    col = _iota2((CHUNK, gw), 1) & (CHUNK - 1)
    row = _iota2((CHUNK, gw), 0)

    def group_cols(x, base):
        out = x[:, base:base + 1]
        for j in range(1, hpg):
            out = jnp.where(blk[0:x.shape[0]] == j, x[:, base + j:base + j + 1], out)
        return out

    def chunk(c, carry):
        dirs = []
        for d in (0, 1):
            rows = _chunk_rows(c, d, nch)
            cum = cum_s[d, rows, :]
            tot = cum[CHUNK - 1:CHUNK, :] if d == 0 else cum[0:1, :]
            cum_t = _rows_of(_one_hot_rows(SSD_HEADS, d * SSD_HEADS), jnp.concatenate([cum] * hpg, axis=0))
            dirs.append(dict(rows=rows, cum=cum, tot=tot, cum_t=cum_t))

        st = []
        for d, g in groups:
            dd = dirs[d]
            rows = dd["rows"]
            base = d * SSD_HEADS + g * hpg
            incl = (col <= row) if d == 0 else (col >= row)
            ccol = group_cols(dd["cum"], base)
            crow = dd["cum_t"][g * hpg:g * hpg + 1, :]
            for j in range(1, hpg):
                crow = jnp.where(blk[0:1] == j, dd["cum_t"][g * hpg + j:g * hpg + j + 1, :], crow)
            tot = group_cols(dd["tot"], base)
            decay = jnp.exp(jnp.where(incl, ccol - crow, -jnp.inf))
            bg = xbc[d, rows, MIX_W + g * SSD_STATE:MIX_W + (g + 1) * SSD_STATE]
            cg = xbc[d, rows, MIX_W + (SSD_GROUPS + g) * SSD_STATE:MIX_W + (SSD_GROUPS + g + 1) * SSD_STATE]
            vg = v_s[d, rows, g * gw:(g + 1) * gw]
            scores = _mm_nt(cg, jnp.concatenate([bg] * hpg, axis=0))
            inter = _mm(cg, s_ref[d, g])
            v_bd = jnp.concatenate([jnp.where(blk == j, vg, 0.0) for j in range(hpg)], axis=0)
            st.append(dict(p=scores * decay, inter=inter * jnp.exp(ccol), v_bd=v_bd, bg=bg,
                           v_end=vg * jnp.exp(tot - ccol), e_tot=jnp.exp(tot)))

        for (d, g), s in zip(groups, st):
            rows = dirs[d]["rows"]
            y = yf if d == 0 else yb
            yg = _mm(s["p"], s["v_bd"]) + s["inter"]
            if d == 0:
                yg = yg + xbc[d, rows, g * gw:(g + 1) * gw] * dskip[:, g * gw:(g + 1) * gw]
            y[rows, g * gw:(g + 1) * gw] = yg
            s_ref[d, g] = s_ref[d, g] * s["e_tot"] + _mm_tn(s["bg"], s["v_end"])
        return carry

    lax.fori_loop(0, nch, chunk, 0, unroll=4)


def _ssd_scan(p, cw, cb, dtb, nega, dskip, state):
    def scratch(geo):
        return [pltpu.VMEM((geo.tb + 2 * CONV_HALO, LANES), F32),
                pltpu.VMEM((2, geo.tb, 2 * MIX_W), F32),
                pltpu.VMEM((2, geo.tb, LANES), F32),
                pltpu.VMEM((2, geo.tb, MIX_W), F32)]
    return _scan_call(_ssd_kernel, p, (cw, cb, dtb, nega, dskip), state, scratch, False, (MIX_W, 2 * MIX_W), "ssd_scan")


def _ssd_state(bsz):
    return (jnp.zeros((bsz, 2, SSD_GROUPS, SSD_STATE, MIX_W // SSD_GROUPS), F32),)


def _mlstm_kernel(geo, pf, pb, ib, fb, c_in, m_in, yf, yb, c_ref, m_ref, qkv, b_s, li_s, cm_s, ysc):
    i = pl.program_id(1)
    tb = geo.tb
    nch = tb // CHUNK
    nh, dk, dv = MLSTM_HEADS, MLSTM_K, MLSTM_V
    kscale = dk ** -0.5
    k0, v0, g0 = nh * dk, 2 * nh * dk, 2 * nh * dk + 2 * MIX_W
    fl = 2 * nh

    @pl.when(i == 0)
    def _():
        c_ref[...] = c_in[...]
        m_ref[...] = m_in[...]

    for d, p in enumerate((pf, pb)):
        _stage(qkv.at[d], 0, p, 0, g0 - MIX_W, geo)
        _stage(li_s.at[d], 0, p, g0, g0 + LANES, geo)
        gates = li_s[d]
        li = pltpu.roll(gates + ib[...], fl, axis=1)
        b = _chunk_cumsum(_log_sigmoid(gates + fb[...]), d)
        li_s[d] = li
        b_s[d] = b
        cm_s[d] = _chunk_scan(li - b, d, jnp.maximum, -jnp.inf)

    ones_col = jnp.where(_iota2((CHUNK, dv), 1) == 0, 1.0, 0.0).astype(F32)
    npair = nh // 2
    pairs = [(d, hp) for d in (0, 1) for hp in range(npair)]
    lane_hi = _iota2((CHUNK, 2 * CHUNK), 1) >= CHUNK
    col = _iota2((CHUNK, 2 * CHUNK), 1) & (CHUNK - 1)
    row = _iota2((CHUNK, 2 * CHUNK), 0)
    ew = 2 * dv
    ydst = _y_targets(geo, yf, yb, ysc)

    def chunk(c, carry):
        dirs = []
        for d in (0, 1):
            rows = _chunk_rows(c, d, nch)
            sel = _one_hot_rows(8, fl + d * nh)
            b = b_s[d, rows, :]
            li = li_s[d, rows, :]
            tot = b[CHUNK - 1:CHUNK, :] if d == 0 else b[0:1, :]
            m_row = m_ref[d:d + 1, :]
            end_log = tot - b + li
            m_new = jnp.maximum(tot + m_row, jnp.max(end_log, axis=0, keepdims=True))
            m_ref[d:d + 1, :] = m_new
            inter = b + m_row
            bli = b - li
            m_t = jnp.maximum(inter, b + cm_s[d, rows, :])
            dirs.append(dict(rows=rows, m_t=m_t, bm=b - m_t, w_inter=jnp.exp(inter - m_t),
                             bli_t=_rows_of(sel, jnp.concatenate([bli, bli], axis=0)),
                             w_end=jnp.exp(end_log - m_new), keep=jnp.exp(tot + m_row - m_new)))

        st = []
        for d, hp in pairs:
            g = dirs[d]
            la = fl + d * nh + 2 * hp
            incl = (col <= row) if d == 0 else (col >= row)
            bli_row = jnp.where(lane_hi[0:1], g["bli_t"][2 * hp + 1:2 * hp + 2, :], g["bli_t"][2 * hp:2 * hp + 1, :])
            gate = jnp.exp(jnp.where(incl, _pair_cols(g["bm"], lane_hi, la) - bli_row, -jnp.inf))
            q =qkv[d, g["rows"], hp * 2 * dk:(hp + 1) * 2 * dk]
            k = qkv[d, g["rows"], k0 + hp * 2 * dk:k0 + (hp + 1) * 2 * dk] * kscale
            vext = [jnp.concatenate([qkv[d, g["rows"], v0 + (2 * hp + h) * dv:v0 + (2 * hp + h + 1) * dv], ones_col],
                                    axis=1) for h in (0, 1)]
            scores = _mm_nt(q, _block_diag2(k, lane_hi)) * gate
            qw = q * _pair_cols(g["w_inter"], lane_hi, la)
            st.append(dict(lhs=jnp.concatenate([scores, qw], axis=1), k=k, vext=vext))

        res = [_mm(s["lhs"], jnp.concatenate([_block_diag_wide(s["vext"][0], s["vext"][1]), c_ref[d, hp]], axis=0))
               for (d, hp), s in zip(pairs, st)]

        for (d, hp), s, r in zip(pairs, st, res):
            g = dirs[d]
            la = fl + d * nh + 2 * hp
            kw = s["k"] * _pair_cols(g["w_end"], lane_hi, la)
            for h in (0, 1):
                den = jnp.maximum(jnp.abs(r[:, h * ew + dv:h * ew + dv + 1]), jnp.exp(-g["m_t"][:, la + h:la + h + 1]))
                ydst[d][g["rows"], (2 * hp + h) * dv:(2 * hp + h + 1) * dv] = r[:, h * ew:h * ew + dv] / den
                kw_h = jnp.where(lane_hi, kw, 0.0) if h == 1 else jnp.where(lane_hi, 0.0, kw)
                cs = slice(h * ew, (h + 1) * ew)
                c_ref[d, hp, :, cs] = c_ref[d, hp, :, cs] * g["keep"][:, la + h:la + h + 1] + _mm_tn(kw_h, s["vext"][h])
        return carry

    lax.fori_loop(0, nch, chunk, 0, unroll=4)
    _y_finish(geo, yf, yb, ysc)


def _mlstm_scan(p, ib, fb, state, colmajor):
    def scratch(geo):
        return [pltpu.VMEM((2, geo.tb, 2 * MLSTM_HEADS * MLSTM_K + MIX_W), F32),
                pltpu.VMEM((2, geo.tb, LANES), F32),
                pltpu.VMEM((2, geo.tb, LANES), F32),
                pltpu.VMEM((2, geo.tb, LANES), F32),
                pltpu.VMEM((2, geo.tb, MIX_W) if geo.colmajor else (2, 8, LANES), F32)]
    return _scan_call(_mlstm_kernel, p, (ib, fb), state, scratch, colmajor, None, "mlstm_scan")


def _mlstm_state(bsz):
    return (jnp.zeros((bsz, 2, MLSTM_HEADS // 2, 2 * MLSTM_K, 4 * MLSTM_V), F32), jnp.zeros((bsz, 8, LANES), F32))


GDN_PREP_CHUNKS = 4


def _gdn_kernel(geo, pf, pfp, pfn, pb, pbp, pbn, cw, dtb, nega, s_in, yf, yb,
                s_ref, xpad, qkv, cum_s, beta_s, lvl, sol_s, attn_s, ysc):
    i = pl.program_id(1)
    tb = geo.tb
    nch = tb // CHUNK
    nh, hd = GDN_HEADS, GDN_HEAD_DIM
    g0 = 4 * MIX_W
    npair = nh // 2
    pw = 2 * hd

    @pl.when(i == 0)
    def _():
        s_ref[...] = s_in[...]
        r, c = _iota2((CHUNK, 2 * CHUNK), 0), _iota2((CHUNK, 2 * CHUNK), 1) & (CHUNK - 1)
        for k in range(lvl.shape[0]):
            joined = ((r >> (k + 1)) == (c >> (k + 1))) & (((r >> k) & 1) != ((c >> k) & 1))
            lvl[k] = jnp.where(joined, 1.0, 0.0).astype(F32)

    for d, (p, pp, pn) in enumerate(((pf, pfp, pfn), (pb, pbp, pbn))):
        j = i if d == 0 else geo.nblk - 1 - i
        _conv_silu(qkv.at[d], xpad, p, pp, pn, 0, 3 * MIX_W, cw, None, j != 0, j != geo.nblk - 1, geo)
        for h in range(2 * nh):
            hs = slice(h * hd, (h + 1) * hd)
            x = qkv[d, :, hs]
            x = x * lax.rsqrt(jnp.sum(x * x, axis=-1, keepdims=True) + 1e-6)
            qkv[d, :, hs] = x * (hd ** -0.5) if h < nh else x
        _stage(beta_s.at[d], 0, p, g0, g0 + LANES, geo)
        gates = beta_s[d]
        cum_s[d] = _chunk_cumsum(nega[...] * _softplus(gates + dtb[...]), d)
        beta_s[d] = pltpu.roll(_sigmoid(gates), LANES - 2 * nh, axis=1)

    lane_hi = _iota2((CHUNK, 2 * CHUNK), 1) >= CHUNK
    col = _iota2((CHUNK, 2 * CHUNK), 1) & (CHUNK - 1)
    row = _iota2((CHUNK, 2 * CHUNK), 0)
    ydst = _y_targets(geo, yf, yb, ysc)

    def gate_cols(c, d):
        rows = _chunk_rows(c, d, nch)
        cum = cum_s[d, rows, :]
        tot = cum[CHUNK - 1:CHUNK, :] if d == 0 else cum[0:1, :]
        return rows, cum, tot

    def heads(d, rows, base, hp):
        return [qkv[d, rows, base + (2 * hp + h) * hd:base + (2 * hp + h + 1) * hd] for h in (0, 1)]

    ngroup = min(nch, GDN_PREP_CHUNKS)
    def prep(it, carry):
        dirs = {}
        for cc in range(ngroup):
            for d in (0, 1):
                rows, cum, tot = gate_cols(it * ngroup + cc, d)
                cum_t = _rows_of(_one_hot_rows(8, d * nh), jnp.concatenate([cum, cum], axis=0))
                dirs[cc, d] = dict(rows=rows, cum=cum, beta=beta_s[d, rows, :], cum_t=cum_t, e_cum=jnp.exp(cum))
        units = [(cc, d, hp) for cc in range(ngroup) for d in (0, 1) for hp in range(npair)]

        st = []
        for cc, d, hp in units:
            g = dirs[cc, d]
            la = d * nh + 2 * hp
            incl = (col <= row) if d == 0 else (col >= row)
            strict = (col < row) if d == 0 else (col > row)
            crow = jnp.where(lane_hi[0:1], g["cum_t"][2 * hp + 1:2 * hp + 2, :], g["cum_t"][2 * hp:2 * hp + 1, :])
            decay = jnp.exp(jnp.where(incl, _pair_cols(g["cum"], lane_hi, la) - crow, -jnp.inf))
            q, k, v = (heads(d, g["rows"], base, hp) for base in (0, MIX_W, 2 * MIX_W))
            bcol = [g["beta"][:, la + h:la + h + 1] for h in (0, 1)]
            ecol = [g["e_cum"][:, la + h:la + h + 1] for h in (0, 1)]
            kb = [k[h] * bcol[h] for h in (0, 1)]
            lhs = jnp.concatenate([jnp.concatenate(kb, axis=1), jnp.concatenate(q, axis=1)], axis=0)
            prod = _mm_nt(lhs, _block_diag_wide(k[0], k[1]))
            attn_s[d, hp, g["rows"], :] = prod[CHUNK:2 * CHUNK] * decay
            rhs = [jnp.concatenate([kb[h] * ecol[h], v[h] * bcol[h]], axis=1) for h in (0, 1)]
            st.append(dict(a=jnp.where(strict, prod[0:CHUNK] * decay, 0.0), rhs=rhs))

        es = [-(s["a"] * lvl[0]) for s in st]
        for lv in range(1, lvl.shape[0]):
            aks = [s["a"] * lvl[lv] for s in st]
            xs = [ak + _mm(ak, _block_diag2(e, lane_hi)) for ak, e in zip(aks, es)]
            es = [e - (x + _mm(e, _block_diag2(x, lane_hi))) for e, x in zip(es, xs)]

        for (cc, d, hp), s, e in zip(units, st, es):
            rhs = s["rhs"]
            sol_s[d, hp, dirs[cc, d]["rows"], :] = (jnp.concatenate(rhs, axis=1)
                                                   + _mm(e, _block_diag_wide(rhs[0], rhs[1])))
        return carry

    pairs = [(d, hp) for d in (0, 1) for hp in range(npair)]

    def chunk(c, carry):
        dirs = []
        for d in (0, 1):
            rows, cum, tot = gate_cols(c, d)
            dirs.append(dict(rows=rows, e_cum=jnp.exp(cum), e_end=jnp.exp(tot - cum), e_tot=jnp.exp(tot)))
        rs, sols, ks = [], [], []
        for d, hp in pairs:
            g = dirs[d]
            la = d * nh + 2 * hp
            sol = sol_s[d, hp, g["rows"], :]
            q = heads(d, g["rows"], 0, hp)
            qe = jnp.concatenate([q[h] * g["e_cum"][:, la + h:la + h + 1] for h in (0, 1)], axis=1)
            w = jnp.concatenate([sol[:, 0:hd], sol[:, 2 * hd:3 * hd]], axis=1)
            rs.append(_mm(jnp.concatenate([w, qe], axis=0), s_ref[d, hp]))
            sols.append(sol)
        for (d, hp), sol, r in zip(pairs, sols, rs):
            g = dirs[d]
            la = d * nh + 2 * hp
            k = heads(d, g["rows"], MIX_W, hp)
            v_new = [sol[:, hd:2 * hd] - r[0:CHUNK, 0:hd], sol[:, 3 * hd:4 * hd] - r[0:CHUNK, hd:pw]]
            ydst[d][g["rows"], hp * pw:(hp + 1) * pw] = (
                r[CHUNK:2 * CHUNK] + _mm(attn_s[d, hp, g["rows"], :], _block_diag_wide(v_new[0], v_new[1])))
            for h in (0, 1):
                blk = slice(h * hd, (h + 1) * hd)
                k_end = k[h] * g["e_end"][:, la + h:la + h + 1]
                s_ref[d, hp, blk, blk] = (s_ref[d, hp, blk, blk] * g["e_tot"][:, la + h:la + h + 1]
                                          + _mm_tn(k_end, v_new[h]))
        return carry

    prep(0, 0)
    for grp in range(nch // ngroup):
        if grp + 1 < nch // ngroup:
            prep(grp + 1, 0)
        for c in range(grp * ngroup, (grp + 1) * ngroup):
            chunk(c, 0)
    _y_finish(geo, yf, yb, ysc)


def _gdn_scan(p, cw, dtb, nega, state, colmajor):
    nlevels = CHUNK.bit_length() - 1
    pw = 2 * GDN_HEAD_DIM

    def scratch(geo):
        return [pltpu.VMEM((geo.tb + 2 * CONV_HALO, LANES), F32),
                pltpu.VMEM((2, geo.tb, 3 * MIX_W), F32),
                pltpu.VMEM((2, geo.tb, LANES), F32),
                pltpu.VMEM((2, geo.tb, LANES), F32),
                pltpu.VMEM((nlevels, CHUNK, 2 * CHUNK), F32),
                pltpu.VMEM((2, GDN_HEADS // 2, geo.tb, 2 * pw), F32),
                pltpu.VMEM((2, GDN_HEADS // 2, geo.tb, 2 * CHUNK), F32),
                pltpu.VMEM((2, geo.tb, MIX_W) if geo.colmajor else (2, 8, LANES), F32)]
    return _scan_call(_gdn_kernel, p, (cw, dtb, nega), state, scratch, colmajor, (0, 3 * MIX_W), "gdn_scan")


def _gdn_state(bsz):
    return (jnp.zeros((bsz, 2, GDN_HEADS // 2, 2 * GDN_HEAD_DIM, 2 * GDN_HEAD_DIM), F32),)


def _gla_kernel(geo, pf, pb, up, gb, st_in, yf, yb, st_ref, cum_s):
    i = pl.program_id(1)
    tb = geo.tb
    nch = tb // CHUNK
    nh, dk, dv = GLA_HEADS, GLA_K, GLA_V
    qscale = dk ** -0.5
    q0, k0, v0, g0 = 0, nh * dk, 2 * nh * dk, 2 * nh * dk + 2 * MIX_W
    mid = CHUNK // 2

    @pl.when(i == 0)
    def _():
        st_ref[...] = st_in[...]

    for d, p in enumerate((pf, pb)):
        gk = _mm(p[:, g0:g0 + LANES], up[d]) + gb[d:d + 1, :]
        cum_s[d] = _chunk_cumsum(_log_sigmoid(gk) * (1.0 / GLA_GATE_NORM), d)

    npair = nh // 2
    pairs = [(d, hp) for d in (0, 1) for hp in range(npair)]
    lane_hi = _iota2((CHUNK, 2 * CHUNK), 1) >= CHUNK
    col = _iota2((CHUNK, 2 * CHUNK), 1) & (CHUNK - 1)
    row = _iota2((CHUNK, 2 * CHUNK), 0)
    assert dk == CHUNK

    def chunk(c, carry):
        st = []
        for d, hp in pairs:
            p = pf if d == 0 else pb
            rows = _chunk_rows(c, d, nch)
            ps = slice(hp * 2 * dk, (hp + 1) * 2 * dk)
            cum = cum_s[d, rows, ps]
            ref = cum[mid:mid + 1, :] if d == 0 else cum[CHUNK - 1 - mid:CHUNK - mid, :]
            tot = cum[CHUNK - 1:CHUNK, :] if d == 0 else cum[0:1, :]
            q = p[rows, q0 + hp * 2 * dk:q0 + (hp + 1) * 2 * dk] * qscale
            k = p[rows, k0 + hp * 2 * dk:k0 + (hp + 1) * 2 * dk]
            v = [p[rows, v0 + (2 * hp + h) * dv:v0 + (2 * hp + h + 1) * dv] for h in (0, 1)]
            incl = (col <= row) if d == 0 else (col >= row)
            kg_bd = _block_diag2(k * jnp.exp(ref - cum), lane_hi)
            scores = jnp.where(incl, _mm_nt(q * jnp.exp(cum - ref), kg_bd), 0.0)
            st.append(dict(rows=rows, scores=scores, v=v, qe=q * jnp.exp(cum), ke=k * jnp.exp(tot - cum),
                           e_tot=jnp.exp(tot)))

        for (d, hp), s in zip(pairs, st):
            y = yf if d == 0 else yb
            y[s["rows"], hp * 2 * dv:(hp + 1) * 2 * dv] = (_mm(s["scores"], _block_diag_wide(s["v"][0], s["v"][1]))
                                                          + _mm_nt(s["qe"], st_ref[d, hp]))
            for h in (0, 1):
                ke_h = jnp.where(lane_hi, s["ke"], 0.0) if h == 1 else jnp.where(lane_hi, 0.0, s["ke"])
                rs = slice(h * dv, (h + 1) * dv)
                st_ref[d, hp, rs, :] = st_ref[d, hp, rs, :] * s["e_tot"] + _mm_tn(s["v"][h], ke_h)
        return carry

    lax.fori_loop(0, nch, chunk, 0, unroll=4)


def _gla_scan(p, up, gb, state):
    def scratch(geo):
        return [pltpu.VMEM((2, geo.tb, GLA_HEADS * GLA_K), F32)]
    return _scan_call(_gla_kernel, p, (up, gb), state, scratch, False, None, "gla_scan")


def _gla_state(bsz):
    return (jnp.zeros((bsz, 2, GLA_HEADS // 2, 2 * GLA_V, 2 * GLA_K), F32),)


def _merge_kernel(h_ref, mod_ref, nw_ref,
                  ssd_f, ssd_b, ml_f, ml_b, gd_f, gd_b, gl_f, gl_b,
                  z_ref, o_ref, gg_ref, lg_ref,
                  nssd, nml, ngd, ngl, wg_ref, wb_ref, wo_ref, out_ref):
    x = h_ref[...]
    xm = _norm_modulate(x, nw_ref[...], mod_ref[0:1, :], mod_ref[1:2, :]).astype(BF16)
    branches = (
        _group_rmsnorm((ssd_f[...] + ssd_b[...]) * _silu(z_ref[...]), nssd[...], SSD_GROUPS),
        _group_rmsnorm(ml_f[...] + ml_b[...], nml[...], MLSTM_HEADS) * _sigmoid(o_ref[...]),
        _group_rmsnorm(gd_f[...] + gd_b[...], ngd[...], GDN_HEADS) * _silu(gg_ref[...]),
        _group_rmsnorm(gl_f[...] + gl_b[...], ngl[...], GLA_HEADS) * _silu(lg_ref[...]),
    )
    u = None
    for n, br in enumerate(branches):
        gate = _sigmoid(jnp.dot(xm, wg_ref[:, n * D_MODEL:(n + 1) * D_MODEL], preferred_element_type=F32))
        term = gate * _mm(br, wb_ref[n])
        u = term if u is None else u + term
    out_ref[...] = x + mod_ref[2:3, :] * _mm(u, wo_ref[...])


def _merge(h, mods, norm_w, ys, p_ssd, p_ml, p_gd, p_gl, norms, wg, wb, wo, stream):
    bsz, t, d = h.shape
    grid, tok, mod = _token_grid_specs(bsz, t, stream)
    gate_spec = lambda blk: pl.BlockSpec((None, _token_tile(t), MIX_W), lambda b, i: (b, i, blk))
    return pl.pallas_call(
        _merge_kernel,
        grid=grid,
        in_specs=([tok(d), mod, _const_spec((1, d))] + [tok(MIX_W)] * 8
                  + [gate_spec(0), gate_spec(2), gate_spec(3), gate_spec(2)]
                  + [_const_spec((1, MIX_W))] * 4
                  + [_const_spec(wg.shape), _const_spec(wb.shape), _const_spec(wo.shape)]),
        out_specs=tok(d),
        out_shape=jax.ShapeDtypeStruct((bsz, t, d), F32),
        compiler_params=_TOKEN_PARAMS,
        name="merge_out",
    )(h, mods, norm_w.reshape(1, d), *ys, p_ssd, p_ml, p_gd, p_gl,
      *[n.reshape(1, MIX_W) for n in norms], wg, wb, wo)


def _ffn_body(h_ref, mod_ref, nw_ref, wg_ref, wu_ref, wd_ref):
    x = h_ref[...]
    xm = _norm_modulate(x, nw_ref[...], mod_ref[3:4, :], mod_ref[4:5, :]).astype(BF16)
    a = jnp.dot(xm, wg_ref[...], preferred_element_type=F32)
    b = jnp.dot(xm, wu_ref[...], preferred_element_type=F32)
    return x + mod_ref[5:6, :] * _mm(_silu(a) * b, wd_ref[...])


def _ffn_kernel(h_ref, mod_ref, nw_ref, wg_ref, wu_ref, wd_ref, out_ref):
    out_ref[...] = _ffn_body(h_ref, mod_ref, nw_ref, wg_ref, wu_ref, wd_ref)


def _ffn_final_kernel(h_ref, mod_ref, nw_ref, wg_ref, wu_ref, wd_ref, nf_ref, out_ref):
    y = _ffn_body(h_ref, mod_ref, nw_ref, wg_ref, wu_ref, wd_ref)
    out_ref[...] = y * lax.rsqrt(jnp.mean(y * y, axis=-1, keepdims=True) + 1e-6) * nf_ref[...]


def _ffn(h, mods, norm_w, wg, wu, wd, stream, final_norm_w=None):
    bsz, t, d = h.shape
    grid, tok, mod = _token_grid_specs(bsz, t, stream)
    extra = [] if final_norm_w is None else [final_norm_w.reshape(1, d)]
    return pl.pallas_call(
        _ffn_kernel if final_norm_w is None else _ffn_final_kernel,
        grid=grid,
        in_specs=[tok(d), mod, _const_spec((1, d)), _const_spec(wg.shape), _const_spec(wu.shape),
                  _const_spec(wd.shape)] + [_const_spec((1, d))] * len(extra),
        out_specs=tok(d),
        out_shape=jax.ShapeDtypeStruct((bsz, t, d), F32),
        compiler_params=_TOKEN_PARAMS,
        name="swiglu",
    )(h, mods, norm_w.reshape(1, d), wg, wu, wd, *extra)


def _pad_cols(w, total):
    return jnp.pad(w, ((0, 0), (0, total - w.shape[1])))


def _lane_row(values, lane0):
    v = values.reshape(-1).astype(F32)
    return jnp.zeros((1, LANES), F32).at[0, lane0:lane0 + v.shape[0]].set(v)


def kernel(x, c, ctx, c_ctx, ada_w, ada_b, norm_mix_w, w_in, ssd_conv_w, ssd_conv_b, ssd_dt_bias,
           ssd_a_log, ssd_d, ssd_norm_w, mlstm_i_bias, mlstm_f_bias, mlstm_norm_w, gdn_conv_w, gdn_a_log,
           gdn_dt_bias, gdn_norm_w, gla_gk_up, gla_gk_bias, gla_norm_w, w_branch, w_out, norm_ffn_w,
           ffn_w_gate, ffn_w_up, ffn_w_down, norm_f_w):
    bsz, seq, d = x.shape
    depth = w_in.shape[0]
    assert d == D_MODEL and ctx.shape[1] % TOKEN_BLOCK == 0 and seq % TOKEN_BLOCK == 0

    cond = jnp.stack([jnp.broadcast_to(c_ctx, c.shape), c], axis=1).reshape(2 * bsz, d)
    mods_all = _mods(cond, ada_w, ada_b).reshape(depth, 2 * bsz, 6, d)

    ssd_cols = 2 * MIX_W + 2 * SSD_GROUPS * SSD_STATE + 2 * SSD_HEADS
    ml_cols = 2 * MLSTM_HEADS * MLSTM_K + 2 * MIX_W + 4 * MLSTM_HEADS
    gd_cols = 4 * MIX_W + 4 * GDN_HEADS
    gl_cols = 2 * GLA_HEADS * GLA_K + 2 * MIX_W + 2 * GLA_RANK
    o1, o2, o3, o4 = ssd_cols, ssd_cols + ml_cols, ssd_cols + ml_cols + gd_cols, ssd_cols + ml_cols + gd_cols + gl_cols

    hs = [ctx, x]
    for l in range(depth):
        last = l == depth - 1
        wl = w_in[l]
        ws = (_pad_cols(wl[:, :o1], SSD_P).astype(BF16), _pad_cols(wl[:, o1:o2], MLSTM_P).astype(BF16),
              _pad_cols(wl[:, o2:o3], GDN_P).astype(BF16), _pad_cols(wl[:, o3:o4], GLA_P).astype(BF16))
        mods = mods_all[l]
        ssd_par = (ssd_conv_w[l], ssd_conv_b[l].reshape(1, -1), _lane_row(ssd_dt_bias[l], 0),
                   _lane_row(-jnp.exp(ssd_a_log[l]), 0), jnp.repeat(ssd_d[l], SSD_HEAD_DIM).reshape(1, MIX_W))
        ml_par = (_lane_row(mlstm_i_bias[l], 0), _lane_row(mlstm_f_bias[l], 2 * MLSTM_HEADS))
        gd_par = (gdn_conv_w[l], _lane_row(gdn_dt_bias[l], 0), _lane_row(-jnp.exp(gdn_a_log[l]), 0))
        up = jnp.zeros((2, LANES, GLA_HEADS * GLA_K), F32)
        for dd in range(2):
            up = up.at[dd, dd * GLA_RANK:(dd + 1) * GLA_RANK].set(gla_gk_up[l, dd])
        gl_par = (up.astype(BF16), gla_gk_bias[l])
        merge_w = (wl[:, o4:].astype(BF16), w_branch[l].astype(BF16), w_out[l].astype(BF16))
        ffn_w = (ffn_w_gate[l].astype(BF16), ffn_w_up[l].astype(BF16), ffn_w_down[l].astype(BF16))
        norms = (ssd_norm_w[l], mlstm_norm_w[l], gdn_norm_w[l], gla_norm_w[l])

        states = [_ssd_state(bsz), _mlstm_state(bsz), _gdn_state(bsz), _gla_state(bsz)]
        for stream in (0, 1):
            h = hs[stream]
            ps = _inproj(h, mods, norm_mix_w[l], ws, stream)
            cm = stream == 1
            ssd_y = _ssd_scan(ps[0], *ssd_par, states[0])
            ml_y = _mlstm_scan(ps[1], *ml_par, states[1], cm)
            gd_y = _gdn_scan(ps[2], *gd_par, states[2], cm)
            gl_y = _gla_scan(ps[3], *gl_par, states[3])
            states = [ssd_y[2], ml_y[2], gd_y[2], gl_y[2]]
            if stream == 0 and last:
                continue
            ys = ssd_y[:2] + ml_y[:2] + gd_y[:2] + gl_y[:2]
            h = _merge(h, mods, norm_mix_w[l], ys, *ps, norms, *merge_w, stream)
            hs[stream] = _ffn(h, mods, norm_ffn_w[l], *ffn_w, stream, norm_f_w if last else None)
    return hs[1]
```

```python
import functools
from typing import NamedTuple

import jax
import jax.numpy as jnp
from jax import lax
from jax.experimental import pallas as pl
from jax.experimental.pallas import tpu as pltpu

F32 = jnp.float32
BF16 = jnp.bfloat16
HIGHEST = lax.Precision.HIGHEST

D_MODEL = 1024
GRID_W = 64
CHUNK = 64
CHUNK_SHIFT = CHUNK.bit_length() - 1
CONV_K = 5
CONV_HALO = 8
MIX_W = 512
LANES = 128
TOKEN_BLOCK = 256
TOKEN_TILE = 512
COLS_PER_BLOCK = 8
FFN_HIDDEN = 2816
VMEM_LIMIT = 56 * 1024 * 1024

SSD_HEADS, SSD_HEAD_DIM, SSD_STATE, SSD_GROUPS = 8, 64, 128, 2
MLSTM_HEADS, MLSTM_K, MLSTM_V = 4, 64, 128
GDN_HEADS, GDN_HEAD_DIM = 4, 128
GLA_HEADS, GLA_K, GLA_V, GLA_RANK, GLA_GATE_NORM = 4, 64, 128, 16, 16

SSD_P = 2 * MIX_W + 2 * SSD_GROUPS * SSD_STATE + LANES
MLSTM_P = 2 * MLSTM_HEADS * MLSTM_K + 2 * MIX_W + LANES
GDN_P = 4 * MIX_W + LANES
GLA_P = 2 * GLA_HEADS * GLA_K + 2 * MIX_W + LANES


class Geo(NamedTuple):
    nblk: int
    tb: int
    colmajor: bool
    c: int


def _mm(a, b):
    return jnp.dot(a.astype(BF16), b.astype(BF16), preferred_element_type=F32)


def _mm_nt(a, b):
    return lax.dot_general(a.astype(BF16), b.astype(BF16), (((1,), (1,)), ((), ())),
                           preferred_element_type=F32)


def _mm_tn(a, b):
    return lax.dot_general(a.astype(BF16), b.astype(BF16), (((0,), (0,)), ((), ())),
                           preferred_element_type=F32)


def _chunk_scan(x, d, op, identity):
    n = x.shape[0]
    pos = _iota2(x.shape, 0) & (CHUNK - 1)
    sh = 1
    while sh < CHUNK:
        if d == 0:
            x = op(x, jnp.where(pos >= sh, pltpu.roll(x, sh, axis=0), identity))
        else:
            x = op(x, jnp.where(pos < CHUNK - sh, pltpu.roll(x, n - sh, axis=0), identity))
        sh *= 2
    return x


def _chunk_cumsum(x, d):
    return _chunk_scan(x, d, jnp.add, 0.0)


def _rows_of(sel, x):
    return lax.dot_general(sel, x, (((1,), (1,)), ((), ())), precision=HIGHEST,
                           preferred_element_type=F32)


def _sigmoid(x):
    return 1.0 / (1.0 + jnp.exp(-x))


def _silu(x):
    return x * _sigmoid(x)


def _softplus(x):
    return jnp.maximum(x, 0.0) + jnp.log1p(jnp.exp(-jnp.abs(x)))


def _log_sigmoid(x):
    return jnp.minimum(x, 0.0) - jnp.log1p(jnp.exp(-jnp.abs(x)))


def _iota2(shape, axis):
    return lax.broadcasted_iota(jnp.int32, shape, axis)


def _one_hot_rows(nrows, lane0):
    r, c = _iota2((nrows, LANES), 0), _iota2((nrows, LANES), 1)
    return jnp.where(c == r + lane0, 1.0, 0.0).astype(F32)


def _chunk_rows(c, d, nch):
    r0 = c * CHUNK if d == 0 else (nch - 1 - c) * CHUNK
    return pl.ds(r0 if isinstance(r0, int) else pl.multiple_of(r0, CHUNK), CHUNK)


def _pair_cols(x, lane_hi, la):
    return jnp.where(lane_hi, x[:, la + 1:la + 2], x[:, la:la + 1])


def _block_diag2(x, lane_hi):
    return jnp.concatenate([jnp.where(lane_hi, 0.0, x), jnp.where(lane_hi, x, 0.0)], axis=0)


def _block_diag_wide(a, b):
    z = jnp.zeros_like(a)
    return jnp.concatenate([jnp.concatenate([a, z], axis=1), jnp.concatenate([z, b], axis=1)], axis=0)


def _group_rmsnorm(x, w, groups, eps=1e-6):
    gw = x.shape[-1] // groups
    outs = []
    for g in range(groups):
        xg = x[:, g * gw:(g + 1) * gw]
        ms = jnp.mean(xg * xg, axis=-1, keepdims=True)
        outs.append(xg * lax.rsqrt(ms + eps))
    return jnp.concatenate(outs, axis=-1) * w


def _norm_modulate(x, nw, shift, scale, eps=1e-6):
    ms = jnp.mean(x * x, axis=-1, keepdims=True)
    y = x * lax.rsqrt(ms + eps) * nw
    return y * (1.0 + scale) + shift


def _stage(dst_ref, row0, p_ref, c0, c1, geo):
    if not geo.colmajor:
        dst_ref[row0:row0 + geo.tb, :] = p_ref[:, c0:c1]
    else:
        for j in range(geo.tb // CHUNK):
            dst_ref[row0 + j * CHUNK:row0 + (j + 1) * CHUNK, :] = p_ref[:, j, c0:c1]


def _unstage(y_ref, ysc_ref, geo):
    for j in range(geo.tb // CHUNK):
        y_ref[:, j, :] = ysc_ref[j * CHUNK:(j + 1) * CHUNK, :]


def _conv_silu(dst_ref, xpad_ref, p_ref, prev_ref, next_ref, col0, width, w_ref, b_ref,
               left_valid, right_valid, geo):
    tb = geo.tb
    base = CONV_HALO - CONV_K // 2
    win = CHUNK + 2 * CONV_HALO
    for s in range(width // LANES):
        c0 = col0 + s * LANES
        ocols = slice(s * LANES, (s + 1) * LANES)
        if geo.colmajor:
            prev = prev_ref[:, COLS_PER_BLOCK - 1, c0 - col0:c0 - col0 + LANES]
            nxt = next_ref[:, 0, c0 - col0:c0 - col0 + LANES]
        else:
            prev, nxt = prev_ref[:, c0:c0 + LANES], next_ref[:, c0:c0 + LANES]
        xpad_ref[0:CONV_HALO, :] = jnp.where(left_valid, prev, 0.0)
        _stage(xpad_ref, CONV_HALO, p_ref, c0, c0 + LANES, geo)
        xpad_ref[CONV_HALO + tb:2 * CONV_HALO + tb, :] = jnp.where(right_valid, nxt, 0.0)
        taps = [w_ref[k:k + 1, ocols] for k in range(CONV_K)]
        bias = None if b_ref is None else b_ref[:, ocols]

        def tile(rt, carry):
            r0 = pl.multiple_of(rt * CHUNK, CHUNK)
            acc = taps[0] * xpad_ref[pl.ds(r0 + base, CHUNK), :]
            for k in range(1, CONV_K):
                acc = acc + taps[k] * xpad_ref[pl.ds(r0 + base + k, CHUNK), :]
            if bias is not None:
                acc = acc + bias
            dst_ref[pl.ds(r0, CHUNK), ocols] = _silu(acc)
            return carry

        lax.fori_loop(0, tb // CHUNK, tile, 0, unroll=4)


def _mods_kernel(c_ref, w_ref, b_ref, o_ref):
    o_ref[...] = _mm(_silu(c_ref[...]), w_ref[...]) + b_ref[...]


def _mods(cond, ada_w, ada_b):
    depth, d, n = ada_w.shape
    r = cond.shape[0]
    nt = n // d
    return pl.pallas_call(
        _mods_kernel,
        grid=(depth, nt),
        in_specs=[pl.BlockSpec((r, d), lambda l, j: (0, 0)),
                  pl.BlockSpec((None, d, d), lambda l, j: (l, 0, j)),
                  pl.BlockSpec((None, 1, d), lambda l, j: (l, 0, j))],
        out_specs=pl.BlockSpec((None, r, d), lambda l, j: (l, 0, j)),
        out_shape=jax.ShapeDtypeStruct((depth, r, n), F32),
        name="adaln_mods",
    )(cond, ada_w, ada_b.reshape(depth, 1, n))


def _token_tile(t):
    return TOKEN_TILE if t % TOKEN_TILE == 0 else TOKEN_BLOCK


def _token_grid_specs(bsz, t, stream):
    tm = _token_tile(t)
    tok = lambda width: pl.BlockSpec((None, tm, width), lambda b, i: (b, i, 0))
    mod = pl.BlockSpec((None, 6, D_MODEL), lambda b, i: (2 * b + stream, 0, 0))
    return (bsz, t // tm), tok, mod


def _const_spec(shape):
    zeros = (0,) * len(shape)
    return pl.BlockSpec(shape, lambda b, i: zeros, pipeline_mode=pl.Buffered(1))


_TOKEN_PARAMS = pltpu.CompilerParams(dimension_semantics=("parallel", "parallel"), vmem_limit_bytes=VMEM_LIMIT)


def _inproj_kernel(h_ref, mod_ref, nw_ref, w0, w1, w2, w3, o0, o1, o2, o3):
    xm = _norm_modulate(h_ref[...], nw_ref[...], mod_ref[0:1, :], mod_ref[1:2, :]).astype(BF16)
    for w, o in ((w0, o0), (w1, o1), (w2, o2), (w3, o3)):
        o[...] = jnp.dot(xm, w[...], preferred_element_type=F32)


def _inproj(h, mods, norm_w, ws, stream):
    bsz, t, d = h.shape
    grid, tok, mod = _token_grid_specs(bsz, t, stream)
    return pl.pallas_call(
        _inproj_kernel,
        grid=grid,
        in_specs=[tok(d), mod, _const_spec((1, d))] + [_const_spec(w.shape) for w in ws],
        out_specs=[tok(w.shape[1]) for w in ws],
        out_shape=[jax.ShapeDtypeStruct((bsz, t, w.shape[1]), F32) for w in ws],
        compiler_params=_TOKEN_PARAMS,
        name="in_proj",
    )(h, mods, norm_w.reshape(1, d), *ws)


def _scan_call(kernel, p, params, states, scratch_fn, colmajor, conv, name):
    bsz, t, c = p.shape
    if colmajor:
        rows = t // GRID_W
        assert rows == CHUNK and GRID_W % COLS_PER_BLOCK == 0
        geo = Geo(GRID_W // COLS_PER_BLOCK, COLS_PER_BLOCK * CHUNK, True, c)
        pv = p.reshape(bsz, rows, GRID_W, c)
        yshape = (bsz, rows, GRID_W, MIX_W)
        last_halo = rows // CONV_HALO - 1

        def specs(blk):
            main = pl.BlockSpec((None, rows, COLS_PER_BLOCK, c), lambda b, i: (b, 0, blk(i), 0))
            if not conv:
                return [main]
            col0, width = conv
            assert col0 % width == 0
            prev = pl.BlockSpec((None, CONV_HALO, COLS_PER_BLOCK, width),
                                lambda b, i: (b, last_halo, jnp.maximum(blk(i) - 1, 0), col0 // width))
            nxt = pl.BlockSpec((None, CONV_HALO, COLS_PER_BLOCK, width),
                               lambda b, i: (b, 0, jnp.minimum(blk(i) + 1, geo.nblk - 1), col0 // width))
            return [main, prev, nxt]

        yspec = lambda blk: pl.BlockSpec((None, rows, COLS_PER_BLOCK, MIX_W), lambda b, i: (b, 0, blk(i), 0))
    else:
        assert t % TOKEN_BLOCK == 0
        tbk = _token_tile(t)
        geo = Geo(t // tbk, tbk, False, c)
        pv = p
        yshape = (bsz, t, MIX_W)
        per_halo = tbk // CONV_HALO
        nhalo = t // CONV_HALO

        def specs(blk):
            main = pl.BlockSpec((None, tbk, c), lambda b, i: (b, blk(i), 0))
            prev = pl.BlockSpec((None, CONV_HALO, c), lambda b, i: (b, jnp.maximum(blk(i) * per_halo - 1, 0), 0))
            nxt = pl.BlockSpec((None, CONV_HALO, c),
                               lambda b, i: (b, jnp.minimum((blk(i) + 1) * per_halo, nhalo - 1), 0))
            return [main, prev, nxt] if conv else [main]

        yspec = lambda blk: pl.BlockSpec((None, tbk, MIX_W), lambda b, i: (b, blk(i), 0))

    fwd = lambda i: i
    rev = lambda i: geo.nblk - 1 - i
    n_p = 3 if conv else 1
    state_specs = [pl.BlockSpec((None,) + s.shape[1:], lambda b, i, n=s.ndim - 1: (b,) + (0,) * n) for s in states]
    yout = jax.ShapeDtypeStruct(yshape, F32)
    outs = pl.pallas_call(
        functools.partial(kernel, geo),
        grid=(bsz, geo.nblk),
        in_specs=specs(fwd) + specs(rev) + [_const_spec(w.shape) for w in params] + state_specs,
        out_specs=[yspec(fwd), yspec(rev)] + state_specs,
        out_shape=[yout, yout] + [jax.ShapeDtypeStruct(s.shape, F32) for s in states],
        scratch_shapes=scratch_fn(geo),
        compiler_params=pltpu.CompilerParams(dimension_semantics=("parallel", "arbitrary"),
                                             vmem_limit_bytes=VMEM_LIMIT),
        name=name,
    )(*([pv] * (2 * n_p)), *params, *states)
    return outs[0].reshape(bsz, t, MIX_W), outs[1].reshape(bsz, t, MIX_W), tuple(outs[2:])


def _y_targets(geo, yf, yb, ysc):
    return (ysc.at[0], ysc.at[1]) if geo.colmajor else (yf, yb)


def _y_finish(geo, yf, yb, ysc):
    if geo.colmajor:
        _unstage(yf, ysc.at[0], geo)
        _unstage(yb, ysc.at[1], geo)


def _ssd_kernel(geo, pf, pfp, pfn, pb, pbp, pbn, cw, cb, dtb, nega, dskip, s_in,
                yf, yb, s_ref, xpad, xbc, cum_s, v_s):
    i = pl.program_id(1)
    tb = geo.tb
    nch = tb // CHUNK
    hd = SSD_HEAD_DIM
    assert hd == CHUNK

    @pl.when(i == 0)
    def _():
        s_ref[...] = s_in[...]

    for d, (p, pp, pn) in enumerate(((pf, pfp, pfn), (pb, pbp, pbn))):
        j = i if d == 0 else geo.nblk - 1 - i
        _conv_silu(xbc.at[d], xpad, p, pp, pn, MIX_W, 2 * MIX_W, cw, cb, j != 0, j != geo.nblk - 1, geo)
        delta = _softplus(p[:, 3 * MIX_W:3 * MIX_W + LANES] + dtb[...])
        cum_s[d] = _chunk_cumsum(nega[...] * delta, d)
        for h in range(SSD_HEADS):
            lane = d * SSD_HEADS + h
            v_s[d, :, h * hd:(h + 1) * hd] = xbc[d, :, h * hd:(h + 1) * hd] * delta[:, lane:lane + 1]

    hpg = SSD_HEADS // SSD_GROUPS
    gw = hpg * hd
    groups = [(d, g) for d in (0, 1) for g in range(SSD_GROUPS)]
    blk = _iota2((CHUNK, gw), 1) >> CHUNK_SHIFT
    col = _iota2((CHUNK, gw), 1) & (CHUNK - 1)
    row = _iota2((CHUNK, gw), 0)

    def group_cols(x, base):
        out = x[:, base:base + 1]
        for j in range(1, hpg):
            out = jnp.where(blk[0:x.shape[0]] == j, x[:, base + j:base + j + 1], out)
        return out

    def chunk(c, carry):
        dirs = []
        for d in (0, 1):
            rows = _chunk_rows(c, d, nch)
            cum = cum_s[d, rows, :]
            tot = cum[CHUNK - 1:CHUNK, :] if d == 0 else cum[0:1, :]
            cum_t = _rows_of(_one_hot_rows(SSD_HEADS, d * SSD_HEADS), jnp.concatenate([cum] * hpg, axis=0))
            dirs.append(dict(rows=rows, cum=cum, tot=tot, cum_t=cum_t))

        st = []
        for d, g in groups:
            dd = dirs[d]
            rows = dd["rows"]
            base = d * SSD_HEADS + g * hpg
            incl = (col <= row) if d == 0 else (col >= row)
            ccol = group_cols(dd["cum"], base)
            crow = dd["cum_t"][g * hpg:g * hpg + 1, :]
            for j in range(1, hpg):
                crow = jnp.where(blk[0:1] == j, dd["cum_t"][g * hpg + j:g * hpg + j + 1, :], crow)
            tot = group_cols(dd["tot"], base)
            decay = jnp.exp(jnp.where(incl, ccol - crow, -jnp.inf))
            bg = xbc[d, rows, MIX_W + g * SSD_STATE:MIX_W + (g + 1) * SSD_STATE]
            cg = xbc[d, rows, MIX_W + (SSD_GROUPS + g) * SSD_STATE:MIX_W + (SSD_GROUPS + g + 1) * SSD_STATE]
            vg = v_s[d, rows, g * gw:(g + 1) * gw]
            scores = _mm_nt(cg, jnp.concatenate([bg] * hpg, axis=0))
            inter = _mm(cg, s_ref[d, g])
            v_bd = jnp.concatenate([jnp.where(blk == j, vg, 0.0) for j in range(hpg)], axis=0)
            st.append(dict(p=scores * decay, inter=inter * jnp.exp(ccol), v_bd=v_bd, bg=bg,
                           v_end=vg * jnp.exp(tot - ccol), e_tot=jnp.exp(tot)))

        for (d, g), s in zip(groups, st):
            rows = dirs[d]["rows"]
            y = yf if d == 0 else yb
            yg = _mm(s["p"], s["v_bd"]) + s["inter"]
            if d == 0:
                yg = yg + xbc[d, rows, g * gw:(g + 1) * gw] * dskip[:, g * gw:(g + 1) * gw]
            y[rows, g * gw:(g + 1) * gw] = yg
            s_ref[d, g] = s_ref[d, g] * s["e_tot"] + _mm_tn(s["bg"], s["v_end"])
        return carry

    lax.fori_loop(0, nch, chunk, 0, unroll=8)


def _ssd_scan(p, cw, cb, dtb, nega, dskip, state):
    def scratch(geo):
        return [pltpu.VMEM((geo.tb + 2 * CONV_HALO, LANES), F32),
                pltpu.VMEM((2, geo.tb, 2 * MIX_W), F32),
                pltpu.VMEM((2, geo.tb, LANES), F32),
                pltpu.VMEM((2, geo.tb, MIX_W), F32)]
    return _scan_call(_ssd_kernel, p, (cw, cb, dtb, nega, dskip), state, scratch, False, (MIX_W, 2 * MIX_W), "ssd_scan")


def _ssd_state(bsz):
    return (jnp.zeros((bsz, 2, SSD_GROUPS, SSD_STATE, MIX_W // SSD_GROUPS), F32),)


def _mlstm_kernel(geo, pf, pb, ib, fb, c_in, m_in, yf, yb, c_ref, m_ref, qkv, b_s, li_s, cm_s, ysc):
    i = pl.program_id(1)
    tb = geo.tb
    nch = tb // CHUNK
    nh, dk, dv = MLSTM_HEADS, MLSTM_K, MLSTM_V
    kscale = dk ** -0.5
    k0, v0, g0 = nh * dk, 2 * nh * dk, 2 * nh * dk + 2 * MIX_W
    fl = 2 * nh

    @pl.when(i == 0)
    def _():
        c_ref[...] = c_in[...]
        m_ref[...] = m_in[...]

    for d, p in enumerate((pf, pb)):
        _stage(qkv.at[d], 0, p, 0, g0 - MIX_W, geo)
        _stage(li_s.at[d], 0, p, g0, g0 + LANES, geo)
        gates = li_s[d]
        li = pltpu.roll(gates + ib[...], fl, axis=1)
        b = _chunk_cumsum(_log_sigmoid(gates + fb[...]), d)
        li_s[d] = li
        b_s[d] = b
        cm_s[d] = _chunk_scan(li - b, d, jnp.maximum, -jnp.inf)

    ones_col = jnp.where(_iota2((CHUNK, dv), 1) == 0, 1.0, 0.0).astype(F32)
    npair = nh // 2
    pairs = [(d, hp) for d in (0, 1) for hp in range(npair)]
    lane_hi = _iota2((CHUNK, 2 * CHUNK), 1) >= CHUNK
    col = _iota2((CHUNK, 2 * CHUNK), 1) & (CHUNK - 1)
    row = _iota2((CHUNK, 2 * CHUNK), 0)
    ew = 2 * dv
    ydst = _y_targets(geo, yf, yb, ysc)

    def chunk(c, carry):
        dirs = []
        for d in (0, 1):
            rows = _chunk_rows(c, d, nch)
            sel = _one_hot_rows(8, fl + d * nh)
            b = b_s[d, rows, :]
            li = li_s[d, rows, :]
            tot = b[CHUNK - 1:CHUNK, :] if d == 0 else b[0:1, :]
            m_row = m_ref[d:d + 1, :]
            end_log = tot - b + li
            m_new = jnp.maximum(tot + m_row, jnp.max(end_log, axis=0, keepdims=True))
            m_ref[d:d + 1, :] = m_new
            inter = b + m_row
            bli = b - li
            m_t = jnp.maximum(inter, b + cm_s[d, rows, :])
            dirs.append(dict(rows=rows, m_t=m_t, bm=b - m_t, w_inter=jnp.exp(inter - m_t),
                             bli_t=_rows_of(sel, jnp.concatenate([bli, bli], axis=0)),
                             w_end=jnp.exp(end_log - m_new), keep=jnp.exp(tot + m_row - m_new)))

        st = []
        for d, hp in pairs:
            g = dirs[d]
            la = fl + d * nh + 2 * hp
            incl = (col <= row) if d == 0 else (col >= row)
            bli_row = jnp.where(lane_hi[0:1], g["bli_t"][2 * hp + 1:2 * hp + 2, :], g["bli_t"][2 * hp:2 * hp + 1, :])
            gate = jnp.exp(jnp.where(incl, _pair_cols(g["bm"], lane_hi, la) - bli_row, -jnp.inf))
            q =qkv[d, g["rows"], hp * 2 * dk:(hp + 1) * 2 * dk]
            k = qkv[d, g["rows"], k0 + hp * 2 * dk:k0 + (hp + 1) * 2 * dk] * kscale
            vext = [jnp.concatenate([qkv[d, g["rows"], v0 + (2 * hp + h) * dv:v0 + (2 * hp + h + 1) * dv], ones_col],
                                    axis=1) for h in (0, 1)]
            scores = _mm_nt(q, _block_diag2(k, lane_hi)) * gate
            qw = q * _pair_cols(g["w_inter"], lane_hi, la)
            st.append(dict(lhs=jnp.concatenate([scores, qw], axis=1), k=k, vext=vext))

        res = [_mm(s["lhs"], jnp.concatenate([_block_diag_wide(s["vext"][0], s["vext"][1]), c_ref[d, hp]], axis=0))
               for (d, hp), s in zip(pairs, st)]

        for (d, hp), s, r in zip(pairs, st, res):
            g = dirs[d]
            la = fl + d * nh + 2 * hp
            kw = s["k"] * _pair_cols(g["w_end"], lane_hi, la)
            for h in (0, 1):
                den = jnp.maximum(jnp.abs(r[:, h * ew + dv:h * ew + dv + 1]), jnp.exp(-g["m_t"][:, la + h:la + h + 1]))
                ydst[d][g["rows"], (2 * hp + h) * dv:(2 * hp + h + 1) * dv] = r[:, h * ew:h * ew + dv] / den
                kw_h = jnp.where(lane_hi, kw, 0.0) if h == 1 else jnp.where(lane_hi, 0.0, kw)
                cs = slice(h * ew, (h + 1) * ew)
                c_ref[d, hp, :, cs] = c_ref[d, hp, :, cs] * g["keep"][:, la + h:la + h + 1] + _mm_tn(kw_h, s["vext"][h])
        return carry

    lax.fori_loop(0, nch, chunk, 0, unroll=8)
    _y_finish(geo, yf, yb, ysc)


def _mlstm_scan(p, ib, fb, state, colmajor):
    def scratch(geo):
        return [pltpu.VMEM((2, geo.tb, 2 * MLSTM_HEADS * MLSTM_K + MIX_W), F32),
                pltpu.VMEM((2, geo.tb, LANES), F32),
                pltpu.VMEM((2, geo.tb, LANES), F32),
                pltpu.VMEM((2, geo.tb, LANES), F32),
                pltpu.VMEM((2, geo.tb, MIX_W) if geo.colmajor else (2, 8, LANES), F32)]
    return _scan_call(_mlstm_kernel, p, (ib, fb), state, scratch, colmajor, None, "mlstm_scan")


def _mlstm_state(bsz):
    return (jnp.zeros((bsz, 2, MLSTM_HEADS // 2, 2 * MLSTM_K, 4 * MLSTM_V), F32), jnp.zeros((bsz, 8, LANES), F32))


GDN_PREP_CHUNKS = 4


def _gdn_kernel(geo, pf, pfp, pfn, pb, pbp, pbn, cw, dtb, nega, s_in, yf, yb,
                s_ref, xpad, qkv, cum_s, beta_s, lvl, sol_s, attn_s, ysc):
    i = pl.program_id(1)
    tb = geo.tb
    nch = tb // CHUNK
    nh, hd = GDN_HEADS, GDN_HEAD_DIM
    g0 = 4 * MIX_W
    npair = nh // 2
    pw = 2 * hd

    @pl.when(i == 0)
    def _():
        s_ref[...] = s_in[...]
        r, c = _iota2((CHUNK, 2 * CHUNK), 0), _iota2((CHUNK, 2 * CHUNK), 1) & (CHUNK - 1)
        for k in range(lvl.shape[0]):
            joined = ((r >> (k + 1)) == (c >> (k + 1))) & (((r >> k) & 1) != ((c >> k) & 1))
            lvl[k] = jnp.where(joined, 1.0, 0.0).astype(F32)

    for d, (p, pp, pn) in enumerate(((pf, pfp, pfn), (pb, pbp, pbn))):
        j = i if d == 0 else geo.nblk - 1 - i
        _conv_silu(qkv.at[d], xpad, p, pp, pn, 0, 3 * MIX_W, cw, None, j != 0, j != geo.nblk - 1, geo)
        for h in range(2 * nh):
            hs = slice(h * hd, (h + 1) * hd)
            x = qkv[d, :, hs]
            x = x * lax.rsqrt(jnp.sum(x * x, axis=-1, keepdims=True) + 1e-6)
            qkv[d, :, hs] = x * (hd ** -0.5) if h < nh else x
        _stage(beta_s.at[d], 0, p, g0, g0 + LANES, geo)
        gates = beta_s[d]
        cum_s[d] = _chunk_cumsum(nega[...] * _softplus(gates + dtb[...]), d)
        beta_s[d] = pltpu.roll(_sigmoid(gates), LANES - 2 * nh, axis=1)

    lane_hi = _iota2((CHUNK, 2 * CHUNK), 1) >= CHUNK
    col = _iota2((CHUNK, 2 * CHUNK), 1) & (CHUNK - 1)
    row = _iota2((CHUNK, 2 * CHUNK), 0)
    ydst = _y_targets(geo, yf, yb, ysc)

    def gate_cols(c, d):
        rows = _chunk_rows(c, d, nch)
        cum = cum_s[d, rows, :]
        tot = cum[CHUNK - 1:CHUNK, :] if d == 0 else cum[0:1, :]
        return rows, cum, tot

    def heads(d, rows, base, hp):
        return [qkv[d, rows, base + (2 * hp + h) * hd:base + (2 * hp + h + 1) * hd] for h in (0, 1)]

    ngroup = min(nch, GDN_PREP_CHUNKS)
    def prep(it, carry):
        dirs = {}
        for cc in range(ngroup):
            for d in (0, 1):
                rows, cum, tot = gate_cols(it * ngroup + cc, d)
                cum_t = _rows_of(_one_hot_rows(8, d * nh), jnp.concatenate([cum, cum], axis=0))
                dirs[cc, d] = dict(rows=rows, cum=cum, beta=beta_s[d, rows, :], cum_t=cum_t, e_cum=jnp.exp(cum))
        units = [(cc, d, hp) for cc in range(ngroup) for d in (0, 1) for hp in range(npair)]

        st = []
        for cc, d, hp in units:
            g = dirs[cc, d]
            la = d * nh + 2 * hp
            incl = (col <= row) if d == 0 else (col >= row)
            strict = (col < row) if d == 0 else (col > row)
            crow = jnp.where(lane_hi[0:1], g["cum_t"][2 * hp + 1:2 * hp + 2, :], g["cum_t"][2 * hp:2 * hp + 1, :])
            decay = jnp.exp(jnp.where(incl, _pair_cols(g["cum"], lane_hi, la) - crow, -jnp.inf))
            q, k, v = (heads(d, g["rows"], base, hp) for base in (0, MIX_W, 2 * MIX_W))
            bcol = [g["beta"][:, la + h:la + h + 1] for h in (0, 1)]
            ecol = [g["e_cum"][:, la + h:la + h + 1] for h in (0, 1)]
            kb = [k[h] * bcol[h] for h in (0, 1)]
            lhs = jnp.concatenate([jnp.concatenate(kb, axis=1), jnp.concatenate(q, axis=1)], axis=0)
            prod = _mm_nt(lhs, _block_diag_wide(k[0], k[1]))
            attn_s[d, hp, g["rows"], :] = prod[CHUNK:2 * CHUNK] * decay
            rhs = [jnp.concatenate([kb[h] * ecol[h], v[h] * bcol[h]], axis=1) for h in (0, 1)]
            st.append(dict(a=jnp.where(strict, prod[0:CHUNK] * decay, 0.0), rhs=rhs))

        es = [-(s["a"] * lvl[0]) for s in st]
        for lv in range(1, lvl.shape[0]):
            aks = [s["a"] * lvl[lv] for s in st]
            xs = [ak + _mm(ak, _block_diag2(e, lane_hi)) for ak, e in zip(aks, es)]
            es = [e - (x + _mm(e, _block_diag2(x, lane_hi))) for e, x in zip(es, xs)]

        for (cc, d, hp), s, e in zip(units, st, es):
            rhs = s["rhs"]
            sol_s[d, hp, dirs[cc, d]["rows"], :] = (jnp.concatenate(rhs, axis=1)
                                                   + _mm(e, _block_diag_wide(rhs[0], rhs[1])))
        return carry

    pairs = [(d, hp) for d in (0, 1) for hp in range(npair)]

    def chunk(c, carry):
        dirs = []
        for d in (0, 1):
            rows, cum, tot = gate_cols(c, d)
            dirs.append(dict(rows=rows, e_cum=jnp.exp(cum), e_end=jnp.exp(tot - cum), e_tot=jnp.exp(tot)))
        rs, sols, ks = [], [], []
        for d, hp in pairs:
            g = dirs[d]
            la = d * nh + 2 * hp
            sol = sol_s[d, hp, g["rows"], :]
            q = heads(d, g["rows"], 0, hp)
            qe = jnp.concatenate([q[h] * g["e_cum"][:, la + h:la + h + 1] for h in (0, 1)], axis=1)
            w = jnp.concatenate([sol[:, 0:hd], sol[:, 2 * hd:3 * hd]], axis=1)
            rs.append(_mm(jnp.concatenate([w, qe], axis=0), s_ref[d, hp]))
            sols.append(sol)
        for (d, hp), sol, r in zip(pairs, sols, rs):
            g = dirs[d]
            la = d * nh + 2 * hp
            k = heads(d, g["rows"], MIX_W, hp)
            v_new = [sol[:, hd:2 * hd] - r[0:CHUNK, 0:hd], sol[:, 3 * hd:4 * hd] - r[0:CHUNK, hd:pw]]
            ydst[d][g["rows"], hp * pw:(hp + 1) * pw] = (
                r[CHUNK:2 * CHUNK] + _mm(attn_s[d, hp, g["rows"], :], _block_diag_wide(v_new[0], v_new[1])))
            for h in (0, 1):
                blk = slice(h * hd, (h + 1) * hd)
                k_end = k[h] * g["e_end"][:, la + h:la + h + 1]
                s_ref[d, hp, blk, blk] = (s_ref[d, hp, blk, blk] * g["e_tot"][:, la + h:la + h + 1]
                                          + _mm_tn(k_end, v_new[h]))
        return carry

    prep(0, 0)
    for grp in range(nch // ngroup):
        if grp + 1 < nch // ngroup:
            prep(grp + 1, 0)
        for c in range(grp * ngroup, (grp + 1) * ngroup):
            chunk(c, 0)
    _y_finish(geo, yf, yb, ysc)


def _gdn_scan(p, cw, dtb, nega, state, colmajor):
    nlevels = CHUNK.bit_length() - 1
    pw = 2 * GDN_HEAD_DIM

    def scratch(geo):
        return [pltpu.VMEM((geo.tb + 2 * CONV_HALO, LANES), F32),
                pltpu.VMEM((2, geo.tb, 3 * MIX_W), F32),
                pltpu.VMEM((2, geo.tb, LANES), F32),
                pltpu.VMEM((2, geo.tb, LANES), F32),
                pltpu.VMEM((nlevels, CHUNK, 2 * CHUNK), F32),
                pltpu.VMEM((2, GDN_HEADS // 2, geo.tb, 2 * pw), F32),
                pltpu.VMEM((2, GDN_HEADS // 2, geo.tb, 2 * CHUNK), F32),
                pltpu.VMEM((2, geo.tb, MIX_W) if geo.colmajor else (2, 8, LANES), F32)]
    return _scan_call(_gdn_kernel, p, (cw, dtb, nega), state, scratch, colmajor, (0, 3 * MIX_W), "gdn_scan")


def _gdn_state(bsz):
    return (jnp.zeros((bsz, 2, GDN_HEADS // 2, 2 * GDN_HEAD_DIM, 2 * GDN_HEAD_DIM), F32),)


def _gla_kernel(geo, pf, pb, up, gb, st_in, yf, yb, st_ref, cum_s):
    i = pl.program_id(1)
    tb = geo.tb
    nch = tb // CHUNK
    nh, dk, dv = GLA_HEADS, GLA_K, GLA_V
    qscale = dk ** -0.5
    q0, k0, v0, g0 = 0, nh * dk, 2 * nh * dk, 2 * nh * dk + 2 * MIX_W
    mid = CHUNK // 2

    @pl.when(i == 0)
    def _():
        st_ref[...] = st_in[...]

    for d, p in enumerate((pf, pb)):
        gk = _mm(p[:, g0:g0 + LANES], up[d]) + gb[d:d + 1, :]
        cum_s[d] = _chunk_cumsum(_log_sigmoid(gk) * (1.0 / GLA_GATE_NORM), d)

    npair = nh // 2
    pairs = [(d, hp) for d in (0, 1) for hp in range(npair)]
    lane_hi = _iota2((CHUNK, 2 * CHUNK), 1) >= CHUNK
    col = _iota2((CHUNK, 2 * CHUNK), 1) & (CHUNK - 1)
    row = _iota2((CHUNK, 2 * CHUNK), 0)
    assert dk == CHUNK

    def chunk(c, carry):
        st = []
        for d, hp in pairs:
            p = pf if d == 0 else pb
            rows = _chunk_rows(c, d, nch)
            ps = slice(hp * 2 * dk, (hp + 1) * 2 * dk)
            cum = cum_s[d, rows, ps]
            ref = cum[mid:mid + 1, :] if d == 0 else cum[CHUNK - 1 - mid:CHUNK - mid, :]
            tot = cum[CHUNK - 1:CHUNK, :] if d == 0 else cum[0:1, :]
            q = p[rows, q0 + hp * 2 * dk:q0 + (hp + 1) * 2 * dk] * qscale
            k = p[rows, k0 + hp * 2 * dk:k0 + (hp + 1) * 2 * dk]
            v = [p[rows, v0 + (2 * hp + h) * dv:v0 + (2 * hp + h + 1) * dv] for h in (0, 1)]
            incl = (col <= row) if d == 0 else (col >= row)
            kg_bd = _block_diag2(k * jnp.exp(ref - cum), lane_hi)
            scores = jnp.where(incl, _mm_nt(q * jnp.exp(cum - ref), kg_bd), 0.0)
            st.append(dict(rows=rows, scores=scores, v=v, qe=q * jnp.exp(cum), ke=k * jnp.exp(tot - cum),
                           e_tot=jnp.exp(tot)))

        for (d, hp), s in zip(pairs, st):
            y = yf if d == 0 else yb
            y[s["rows"], hp * 2 * dv:(hp + 1) * 2 * dv] = (_mm(s["scores"], _block_diag_wide(s["v"][0], s["v"][1]))
                                                          + _mm_nt(s["qe"], st_ref[d, hp]))
            for h in (0, 1):
                ke_h = jnp.where(lane_hi, s["ke"], 0.0) if h == 1 else jnp.where(lane_hi, 0.0, s["ke"])
                rs = slice(h * dv, (h + 1) * dv)
                st_ref[d, hp, rs, :] = st_ref[d, hp, rs, :] * s["e_tot"] + _mm_tn(s["v"][h], ke_h)
        return carry

    lax.fori_loop(0, nch, chunk, 0, unroll=8)


def _gla_scan(p, up, gb, state):
    def scratch(geo):
        return [pltpu.VMEM((2, geo.tb, GLA_HEADS * GLA_K), F32)]
    return _scan_call(_gla_kernel, p, (up, gb), state, scratch, False, None, "gla_scan")


def _gla_state(bsz):
    return (jnp.zeros((bsz, 2, GLA_HEADS // 2, 2 * GLA_V, 2 * GLA_K), F32),)


def _merge_kernel(h_ref, mod_ref, nw_ref,
                  ssd_f, ssd_b, ml_f, ml_b, gd_f, gd_b, gl_f, gl_b,
                  z_ref, o_ref, gg_ref, lg_ref,
                  nssd, nml, ngd, ngl, wg_ref, wb_ref, wo_ref, out_ref):
    x = h_ref[...]
    xm = _norm_modulate(x, nw_ref[...], mod_ref[0:1, :], mod_ref[1:2, :]).astype(BF16)
    branches = (
        _group_rmsnorm((ssd_f[...] + ssd_b[...]) * _silu(z_ref[...]), nssd[...], SSD_GROUPS),
        _group_rmsnorm(ml_f[...] + ml_b[...], nml[...], MLSTM_HEADS) * _sigmoid(o_ref[...]),
        _group_rmsnorm(gd_f[...] + gd_b[...], ngd[...], GDN_HEADS) * _silu(gg_ref[...]),
        _group_rmsnorm(gl_f[...] + gl_b[...], ngl[...], GLA_HEADS) * _silu(lg_ref[...]),
    )
    u = None
    for n, br in enumerate(branches):
        gate = _sigmoid(jnp.dot(xm, wg_ref[:, n * D_MODEL:(n + 1) * D_MODEL], preferred_element_type=F32))
        term = gate * _mm(br, wb_ref[n])
        u = term if u is None else u + term
    out_ref[...] = x + mod_ref[2:3, :] * _mm(u, wo_ref[...])


def _merge(h, mods, norm_w, ys, p_ssd, p_ml, p_gd, p_gl, norms, wg, wb, wo, stream):
    bsz, t, d = h.shape
    grid, tok, mod = _token_grid_specs(bsz, t, stream)
    gate_spec = lambda blk: pl.BlockSpec((None, _token_tile(t), MIX_W), lambda b, i: (b, i, blk))
    return pl.pallas_call(
        _merge_kernel,
        grid=grid,
        in_specs=([tok(d), mod, _const_spec((1, d))] + [tok(MIX_W)] * 8
                  + [gate_spec(0), gate_spec(2), gate_spec(3), gate_spec(2)]
                  + [_const_spec((1, MIX_W))] * 4
                  + [_const_spec(wg.shape), _const_spec(wb.shape), _const_spec(wo.shape)]),
        out_specs=tok(d),
        out_shape=jax.ShapeDtypeStruct((bsz, t, d), F32),
        compiler_params=_TOKEN_PARAMS,
        name="merge_out",
    )(h, mods, norm_w.reshape(1, d), *ys, p_ssd, p_ml, p_gd, p_gl,
      *[n.reshape(1, MIX_W) for n in norms], wg, wb, wo)


def _ffn_body(h_ref, mod_ref, nw_ref, wg_ref, wu_ref, wd_ref):
    x = h_ref[...]
    xm = _norm_modulate(x, nw_ref[...], mod_ref[3:4, :], mod_ref[4:5, :]).astype(BF16)
    a = jnp.dot(xm, wg_ref[...], preferred_element_type=F32)
    b = jnp.dot(xm, wu_ref[...], preferred_element_type=F32)
    return x + mod_ref[5:6, :] * _mm(_silu(a) * b, wd_ref[...])


def _ffn_kernel(h_ref, mod_ref, nw_ref, wg_ref, wu_ref, wd_ref, out_ref):
    out_ref[...] = _ffn_body(h_ref, mod_ref, nw_ref, wg_ref, wu_ref, wd_ref)


def _ffn_final_kernel(h_ref, mod_ref, nw_ref, wg_ref, wu_ref, wd_ref, nf_ref, out_ref):
    y = _ffn_body(h_ref, mod_ref, nw_ref, wg_ref, wu_ref, wd_ref)
    out_ref[...] = y * lax.rsqrt(jnp.mean(y * y, axis=-1, keepdims=True) + 1e-6) * nf_ref[...]


def _ffn(h, mods, norm_w, wg, wu, wd, stream, final_norm_w=None):
    bsz, t, d = h.shape
    grid, tok, mod = _token_grid_specs(bsz, t, stream)
    extra = [] if final_norm_w is None else [final_norm_w.reshape(1, d)]
    return pl.pallas_call(
        _ffn_kernel if final_norm_w is None else _ffn_final_kernel,
        grid=grid,
        in_specs=[tok(d), mod, _const_spec((1, d)), _const_spec(wg.shape), _const_spec(wu.shape),
                  _const_spec(wd.shape)] + [_const_spec((1, d))] * len(extra),
        out_specs=tok(d),
        out_shape=jax.ShapeDtypeStruct((bsz, t, d), F32),
        compiler_params=_TOKEN_PARAMS,
        name="swiglu",
    )(h, mods, norm_w.reshape(1, d), wg, wu, wd, *extra)


def _pad_cols(w, total):
    return jnp.pad(w, ((0, 0), (0, total - w.shape[1])))


def _lane_row(values, lane0):
    v = values.reshape(-1).astype(F32)
    return jnp.zeros((1, LANES), F32).at[0, lane0:lane0 + v.shape[0]].set(v)


def kernel(x, c, ctx, c_ctx, ada_w, ada_b, norm_mix_w, w_in, ssd_conv_w, ssd_conv_b, ssd_dt_bias,
           ssd_a_log, ssd_d, ssd_norm_w, mlstm_i_bias, mlstm_f_bias, mlstm_norm_w, gdn_conv_w, gdn_a_log,
           gdn_dt_bias, gdn_norm_w, gla_gk_up, gla_gk_bias, gla_norm_w, w_branch, w_out, norm_ffn_w,
           ffn_w_gate, ffn_w_up, ffn_w_down, norm_f_w):
    bsz, seq, d = x.shape
    depth = w_in.shape[0]
    assert d == D_MODEL and ctx.shape[1] % TOKEN_BLOCK == 0 and seq % TOKEN_BLOCK == 0

    cond = jnp.stack([jnp.broadcast_to(c_ctx, c.shape), c], axis=1).reshape(2 * bsz, d)
    mods_all = _mods(cond, ada_w, ada_b).reshape(depth, 2 * bsz, 6, d)

    ssd_cols = 2 * MIX_W + 2 * SSD_GROUPS * SSD_STATE + 2 * SSD_HEADS
    ml_cols = 2 * MLSTM_HEADS * MLSTM_K + 2 * MIX_W + 4 * MLSTM_HEADS
    gd_cols = 4 * MIX_W + 4 * GDN_HEADS
    gl_cols = 2 * GLA_HEADS * GLA_K + 2 * MIX_W + 2 * GLA_RANK
    o1, o2, o3, o4 = ssd_cols, ssd_cols + ml_cols, ssd_cols + ml_cols + gd_cols, ssd_cols + ml_cols + gd_cols + gl_cols

    hs = [ctx, x]
    for l in range(depth):
        last = l == depth - 1
        wl = w_in[l]
        ws = (_pad_cols(wl[:, :o1], SSD_P).astype(BF16), _pad_cols(wl[:, o1:o2], MLSTM_P).astype(BF16),
              _pad_cols(wl[:, o2:o3], GDN_P).astype(BF16), _pad_cols(wl[:, o3:o4], GLA_P).astype(BF16))
        mods = mods_all[l]
        ssd_par = (ssd_conv_w[l], ssd_conv_b[l].reshape(1, -1), _lane_row(ssd_dt_bias[l], 0),
                   _lane_row(-jnp.exp(ssd_a_log[l]), 0), jnp.repeat(ssd_d[l], SSD_HEAD_DIM).reshape(1, MIX_W))
        ml_par = (_lane_row(mlstm_i_bias[l], 0), _lane_row(mlstm_f_bias[l], 2 * MLSTM_HEADS))
        gd_par = (gdn_conv_w[l], _lane_row(gdn_dt_bias[l], 0), _lane_row(-jnp.exp(gdn_a_log[l]), 0))
        up = jnp.zeros((2, LANES, GLA_HEADS * GLA_K), F32)
        for dd in range(2):
            up = up.at[dd, dd * GLA_RANK:(dd + 1) * GLA_RANK].set(gla_gk_up[l, dd])
        gl_par = (up.astype(BF16), gla_gk_bias[l])
        merge_w = (wl[:, o4:].astype(BF16), w_branch[l].astype(BF16), w_out[l].astype(BF16))
        ffn_w = (ffn_w_gate[l].astype(BF16), ffn_w_up[l].astype(BF16), ffn_w_down[l].astype(BF16))
        norms = (ssd_norm_w[l], mlstm_norm_w[l], gdn_norm_w[l], gla_norm_w[l])

        states = [_ssd_state(bsz), _mlstm_state(bsz), _gdn_state(bsz), _gla_state(bsz)]
        for stream in (0, 1):
            h = hs[stream]
            ps = _inproj(h, mods, norm_mix_w[l], ws, stream)
            cm = stream == 1
            ssd_y = _ssd_scan(ps[0], *ssd_par, states[0])
            ml_y = _mlstm_scan(ps[1], *ml_par, states[1], cm)
            gd_y = _gdn_scan(ps[2], *gd_par, states[2], cm)
            gl_y = _gla_scan(ps[3], *gl_par, states[3])
            states = [ssd_y[2], ml_y[2], gd_y[2], gl_y[2]]
            if stream == 0 and last:
                continue
            ys = ssd_y[:2] + ml_y[:2] + gd_y[:2] + gl_y[:2]
            h = _merge(h, mods, norm_mix_w[l], ys, *ps, norms, *merge_w, stream)
            hs[stream] = _ffn(h, mods, norm_ffn_w[l], *ffn_w, stream, norm_f_w if last else None)
    return hs[1]
```
